```python
import jax, jax.numpy as jnp
from jax import lax
import numpy as np

D_MODEL = 4096
BATCH = 8
SEQ = 2048
DEPTH = 1
DEC_BATCH = 16
DEC_SEQ = 32
PAST_LEN = 1024

CHUNK = 64
HEAD_DIM = 128
A_HEADS = D_MODEL // (2 * HEAD_DIM)
B_HEADS = D_MODEL // (2 * HEAD_DIM)
A_WIDTH = A_HEADS * HEAD_DIM
B_WIDTH = B_HEADS * HEAD_DIM
CONV_W = 4
FOX_QBLK = 128
FOX_F_BIAS_INIT = 4.0
PEER_HEADS = 8
PEER_NKEYS = 128
N_EXPERTS = PEER_NKEYS * PEER_NKEYS
PEER_TOPK = 16
PEER_DKEY = 256
PEER_BLK = 64
DN_ALPHA = (2 * DEPTH) ** 0.25
DN_BETA = (8 * DEPTH) ** -0.25
LN_EPS = 1e-5
RMS_EPS = 1e-6
OFF_A_Z = 3 * A_WIDTH
OFF_A_A = OFF_A_Z + A_WIDTH
OFF_A_B = OFF_A_A + A_HEADS
OFF_B_QKV = OFF_A_B + A_HEADS
OFF_B_F = OFF_B_QKV + 3 * B_WIDTH
PROJ_COLS = OFF_B_F + B_HEADS

kernel_name = 'hybrid_gdn_fox_peer_stream_step'


def layer_norm(x, g, b):
    xf = x.astype(jnp.float32)
    mu = jnp.mean(xf, axis=-1, keepdims=True)
    var = jnp.mean(jnp.square(xf - mu), axis=-1, keepdims=True)
    return ((xf - mu) * lax.rsqrt(var + LN_EPS) * g.astype(jnp.float32) + b.astype(jnp.float32)).astype(x.dtype)


def l2_normalize(x):
    return x * lax.rsqrt(jnp.sum(x * x, axis=-1, keepdims=True) + 1e-6)


def causal_dwconv(xfull, w):
    c = xfull.shape[-1]
    return lax.conv_general_dilated(xfull, w[:, None, :], window_strides=(1,), padding='VALID',
                                    dimension_numbers=('NWC', 'WIO', 'NWC'), feature_group_count=c)


def chunked_gated_delta(q, k, v, g, beta, s0, chunk):
    bn, t, h, _ = q.shape
    n = t // chunk

    def blocks(a):
        a = a.reshape((bn, n, chunk) + a.shape[2:])
        return jnp.moveaxis(jnp.moveaxis(a, 2, 3), 1, 0)

    qc, kc, vc, gc, bc = blocks(q), blocks(k), blocks(v), blocks(g), blocks(beta)
    cum = jnp.cumsum(gc, axis=-1)
    idx = jnp.arange(chunk)
    causal = idx[:, None] >= idx[None, :]
    strict = idx[:, None] > idx[None, :]
    decay = jnp.exp(jnp.where(causal, cum[..., :, None] - cum[..., None, :], -jnp.inf))
    kk = jnp.einsum('nbhik,nbhjk->nbhij', kc, kc)
    lower = jnp.where(strict, bc[..., :, None] * kk * decay, 0.0)
    eye = jnp.eye(chunk, dtype=jnp.float32)
    tinv = lax.linalg.triangular_solve(eye + lower, jnp.broadcast_to(eye, lower.shape),
                                       left_side=True, lower=True, unit_diagonal=True)
    w_v = jnp.einsum('nbhij,nbhjv->nbhiv', tinv, bc[..., None] * vc)
    w_k = jnp.einsum('nbhij,nbhjk->nbhik', tinv, (bc * jnp.exp(cum))[..., None] * kc)
    qk = jnp.einsum('nbhik,nbhjk->nbhij', qc, kc) * decay
    q_dec = qc * jnp.exp(cum)[..., None]
    k_dec = kc * jnp.exp(cum[..., -1:] - cum)[..., None]
    g_tot = jnp.exp(cum[..., -1])[..., None, None]

    def step(s, xs):
        w_v_n, w_k_n, qk_n, q_dec_n, k_dec_n, g_n = xs
        u = w_v_n - jnp.einsum('bhik,bhkv->bhiv', w_k_n, s)
        o = jnp.einsum('bhik,bhkv->bhiv', q_dec_n, s) + jnp.einsum('bhij,bhjv->bhiv', qk_n, u)
        s = s * g_n + jnp.einsum('bhik,bhiv->bhkv', k_dec_n, u)
        return s, o

    s_fin, o = lax.scan(step, s0, (w_v, w_k, qk, q_dec, k_dec, g_tot))
    o = jnp.swapaxes(jnp.moveaxis(o, 0, 1), 2, 3).reshape(bn, t, h, o.shape[-1])
    return o, s_fin


def gdn_mixer(qkv_raw, z, a_raw, b_raw, conv_hist, s0, conv_w, a_log, dt_bias, norm_w):
    f32 = jnp.float32
    bn, t, _ = qkv_raw.shape
    full = jnp.concatenate([conv_hist.astype(qkv_raw.dtype), qkv_raw], axis=1)
    new_conv = full[:, -(CONV_W - 1):]
    qkv = jax.nn.silu(causal_dwconv(full.astype(f32), conv_w.astype(f32)))
    q, k, v = jnp.split(qkv, 3, axis=-1)
    q = l2_normalize(q.reshape(bn, t, A_HEADS, HEAD_DIM)) * (HEAD_DIM ** -0.5)
    k = l2_normalize(k.reshape(bn, t, A_HEADS, HEAD_DIM))
    v = v.reshape(bn, t, A_HEADS, HEAD_DIM)
    g = -jnp.exp(a_log.astype(f32)) * jax.nn.softplus(a_raw.astype(f32) + dt_bias.astype(f32))
    beta = jax.nn.sigmoid(b_raw.astype(f32))
    o, s_new = chunked_gated_delta(q, k, v, g, beta, s0.astype(f32), min(CHUNK, t))
    o = o * lax.rsqrt(jnp.mean(o * o, axis=-1, keepdims=True) + RMS_EPS) * norm_w.astype(f32)
    o = o * jax.nn.silu(z.astype(f32).reshape(bn, t, A_HEADS, HEAD_DIM))
    return o.reshape(bn, t, A_WIDTH).astype(qkv_raw.dtype), s_new.astype(s0.dtype), new_conv


def fox_block(q, q_pos, c_q, k, v, c_k):
    s = jnp.einsum('bqhd,bkhd->bhqk', q, k, preferred_element_type=jnp.float32) * (HEAD_DIM ** -0.5)
    bias = jnp.swapaxes(c_q, 1, 2)[..., :, None] - jnp.swapaxes(c_k, 1, 2)[..., None, :]
    mask = jnp.arange(k.shape[1])[None, :] <= q_pos[:, None]
    p = jax.nn.softmax(jnp.where(mask, s + bias, -jnp.inf), axis=-1)
    return jnp.einsum('bhqk,bkhd->bqhd', p.astype(v.dtype), v)


def fox_prompt(q, k, v, logf):
    bn, s, h, d = q.shape
    c = jnp.cumsum(logf, axis=1)
    nb = s // FOX_QBLK
    qb = jnp.swapaxes(q.reshape(bn, nb, FOX_QBLK, h, d), 0, 1)
    cb = jnp.swapaxes(c.reshape(bn, nb, FOX_QBLK, h), 0, 1)
    pos = jnp.arange(nb)[:, None] * FOX_QBLK + jnp.arange(FOX_QBLK)[None, :]
    ob = lax.map(lambda a: fox_block(a[0], a[2], a[1], k, v, c), (qb, cb, pos))
    return jnp.swapaxes(ob, 0, 1).reshape(bn, s, h, d)


def fox_continue(q, k, v, logf, ck, cv, clf):
    p_len, t = ck.shape[1], q.shape[1]
    k_all = jnp.concatenate([ck.astype(k.dtype), k], axis=1)
    v_all = jnp.concatenate([cv.astype(v.dtype), v], axis=1)
    c = jnp.cumsum(jnp.concatenate([clf.astype(jnp.float32), logf], axis=1), axis=1)
    return fox_block(q, p_len + jnp.arange(t), c[:, p_len:], k_all, v_all, c)


def peer_ffn(h, w_q, sub_keys, u_tab, v_tab):
    bn, t, d = h.shape
    m = bn * t
    xf = h.reshape(m, d)
    q = jnp.einsum('md,dc->mc', xf, w_q).reshape(m, PEER_HEADS, 2, PEER_DKEY // 2)
    sc = jnp.einsum('mhpc,hpnc->mhpn', q, sub_keys).astype(jnp.float32)
    s1, i1 = lax.top_k(sc[:, :, 0], PEER_TOPK)
    s2, i2 = lax.top_k(sc[:, :, 1], PEER_TOPK)
    cand = (s1[..., :, None] + s2[..., None, :]).reshape(m, PEER_HEADS, PEER_TOPK * PEER_TOPK)
    cidx = (i1[..., :, None] * PEER_NKEYS + i2[..., None, :]).reshape(m, PEER_HEADS, PEER_TOPK * PEER_TOPK)
    top_s, pos = lax.top_k(cand, PEER_TOPK)
    eidx = jnp.take_along_axis(cidx, pos, axis=-1).reshape(m, PEER_HEADS * PEER_TOPK)
    gate = jax.nn.softmax(top_s, axis=-1).reshape(m, PEER_HEADS * PEER_TOPK)
    nblk = -(-m // PEER_BLK)
    pad = nblk * PEER_BLK - m
    xp = jnp.pad(xf, ((0, pad), (0, 0))).reshape(nblk, PEER_BLK, d)
    ip = jnp.pad(eidx, ((0, pad), (0, 0))).reshape(nblk, PEER_BLK, PEER_HEADS * PEER_TOPK)
    gp = jnp.pad(gate, ((0, pad), (0, 0))).reshape(nblk, PEER_BLK, PEER_HEADS * PEER_TOPK)

    def expert_block(args):
        xb, ib, gb = args
        pre = jnp.einsum('med,md->me', jnp.take(u_tab, ib, axis=0), xb).astype(jnp.float32)
        act = gb * jax.nn.gelu(pre, approximate=False)
        return jnp.einsum('me,med->md', act.astype(v_tab.dtype), jnp.take(v_tab, ib, axis=0))

    out = lax.map(expert_block, (xp, ip, gp))
    return out.reshape(nblk * PEER_BLK, d)[:m].reshape(bn, t, d).astype(h.dtype)


def trunk_layer(x, conv_hist, s0, fox_cache, w_in, conv_w, a_log, dt_bias, gdn_norm_w, fox_f_bias,
                w_out, ln1_g, ln1_b, peer_w_q, peer_sub_keys, peer_u, peer_v, ln2_g, ln2_b):
    bn, t, _ = x.shape
    proj = jnp.einsum('btd,dc->btc', x, w_in)
    a_qkv, a_z, a_a, a_b, b_qkv, b_f = jnp.split(proj, [OFF_A_Z, OFF_A_A, OFF_A_B, OFF_B_QKV, OFF_B_F], axis=-1)
    o_a, s_new, conv_new = gdn_mixer(a_qkv, a_z, a_a, a_b, conv_hist, s0, conv_w, a_log, dt_bias, gdn_norm_w)
    q, k, v = [u.reshape(bn, t, B_HEADS, HEAD_DIM) for u in jnp.split(b_qkv, 3, axis=-1)]
    logf = jax.nn.log_sigmoid(b_f.astype(jnp.float32) + fox_f_bias.astype(jnp.float32))
    if fox_cache is None:
        o_b = fox_prompt(q, k, v, logf)
    else:
        o_b = fox_continue(q, k, v, logf, fox_cache[0], fox_cache[1], fox_cache[2])
    mixed = jnp.einsum('btc,cd->btd', jnp.concatenate([o_a, o_b.reshape(bn, t, B_WIDTH).astype(o_a.dtype)], axis=-1), w_out)
    hid = layer_norm(DN_ALPHA * x + mixed, ln1_g, ln1_b)
    y = layer_norm(DN_ALPHA * hid + peer_ffn(hid, peer_w_q, peer_sub_keys, peer_u, peer_v), ln2_g, ln2_b)
    return y, (k, v, logf, s_new, conv_new)


def setup_inputs(seed: int = 0) -> dict:
    key = jax.random.key(seed)
    ks = jax.random.split(key, 32)
    f32 = jnp.float32

    def nrm(k, shape, scale):
        return jax.random.normal(k, shape, f32) * scale

    col_scale = np.ones((PROJ_COLS,), np.float32)
    col_scale[2 * A_WIDTH:3 * A_WIDTH] = DN_BETA
    col_scale[OFF_B_QKV + 2 * B_WIDTH:OFF_B_F] = DN_BETA
    col_scale = jnp.asarray(col_scale)
    dt = jnp.exp(jax.random.uniform(ks[7], (DEPTH, A_HEADS), f32, np.log(1e-3), np.log(1e-1)))
    return {
        'x_prompt': nrm(ks[0], (BATCH, SEQ, D_MODEL), 1.0),
        'x_sample': nrm(ks[1], (DEC_BATCH, DEC_SEQ, D_MODEL), 1.0),
        'cache_fox_k': nrm(ks[2], (DEPTH, DEC_BATCH, PAST_LEN, B_HEADS, HEAD_DIM), 1.0),
        'cache_fox_v': nrm(ks[3], (DEPTH, DEC_BATCH, PAST_LEN, B_HEADS, HEAD_DIM), DN_BETA),
        'cache_fox_logf': jax.nn.log_sigmoid(FOX_F_BIAS_INIT + nrm(ks[4], (DEPTH, DEC_BATCH, PAST_LEN, B_HEADS), 1.0)),
        'state_gdn': nrm(ks[5], (DEPTH, DEC_BATCH, A_HEADS, HEAD_DIM, HEAD_DIM), 0.1),
        'state_gdn_conv': nrm(ks[6], (DEPTH, DEC_BATCH, CONV_W - 1, 3 * A_WIDTH), 1.0) * col_scale[:3 * A_WIDTH],
        'w_in': nrm(ks[8], (DEPTH, D_MODEL, PROJ_COLS), D_MODEL ** -0.5) * col_scale,
        'gdn_conv_w': nrm(ks[9], (DEPTH, CONV_W, 3 * A_WIDTH), CONV_W ** -0.5),
        'gdn_a_log': jnp.log(jax.random.uniform(ks[10], (DEPTH, A_HEADS), f32, 1.0, 16.0)),
        'gdn_dt_bias': dt + jnp.log(-jnp.expm1(-dt)),
        'gdn_norm_w': 1.0 + nrm(ks[11], (DEPTH, HEAD_DIM), 0.02),
        'fox_f_bias': FOX_F_BIAS_INIT + nrm(ks[12], (DEPTH, B_HEADS), 0.1),
        'w_out': nrm(ks[13], (DEPTH, D_MODEL, D_MODEL), DN_BETA * D_MODEL ** -0.5),
        'ln1_g': 1.0 + nrm(ks[14], (DEPTH, D_MODEL), 0.02),
        'ln1_b': nrm(ks[15], (DEPTH, D_MODEL), 0.02),
        'peer_w_q': nrm(ks[16], (DEPTH, D_MODEL, PEER_HEADS * PEER_DKEY), D_MODEL ** -0.5),
        'peer_sub_keys': nrm(ks[17], (DEPTH, PEER_HEADS, 2, PEER_NKEYS, PEER_DKEY // 2), (PEER_DKEY // 2) ** -0.5),
        'peer_u': nrm(ks[18], (DEPTH, N_EXPERTS, D_MODEL), DN_BETA * D_MODEL ** -0.5),
        'peer_v': nrm(ks[19], (DEPTH, N_EXPERTS, D_MODEL), DN_BETA * (PEER_HEADS * PEER_TOPK) ** -0.5),
        'ln2_g': 1.0 + nrm(ks[20], (DEPTH, D_MODEL), 0.02),
        'ln2_b': nrm(ks[21], (DEPTH, D_MODEL), 0.02),
    }


def reference(x_prompt, x_sample, cache_fox_k, cache_fox_v, cache_fox_logf, state_gdn, state_gdn_conv,
              w_in, gdn_conv_w, gdn_a_log, gdn_dt_bias, gdn_norm_w, fox_f_bias, w_out, ln1_g, ln1_b,
              peer_w_q, peer_sub_keys, peer_u, peer_v, ln2_g, ln2_b):
    n_p = x_prompt.shape[0]
    yp, ys = x_prompt, x_sample
    outs_p = ([], [], [], [], [])
    outs_s = ([], [], [], [], [])
    for l in range(DEPTH):
        wl = (w_in[l], gdn_conv_w[l], gdn_a_log[l], gdn_dt_bias[l], gdn_norm_w[l], fox_f_bias[l], w_out[l],
              ln1_g[l], ln1_b[l], peer_w_q[l], peer_sub_keys[l], peer_u[l], peer_v[l], ln2_g[l], ln2_b[l])
        conv0 = jnp.zeros((n_p, CONV_W - 1, 3 * A_WIDTH), yp.dtype)
        s0 = jnp.zeros((n_p, A_HEADS, HEAD_DIM, HEAD_DIM), yp.dtype)
        yp, st_p = trunk_layer(yp, conv0, s0, None, *wl)
        ys, st_s = trunk_layer(ys, state_gdn_conv[l], state_gdn[l],
                               (cache_fox_k[l], cache_fox_v[l], cache_fox_logf[l]), *wl)
        for lst, arr in zip(outs_p, st_p):
            lst.append(arr)
        for lst, arr in zip(outs_s, st_s):
            lst.append(arr)
    fk_p, fv_p, fl_p, sg_p, sc_p = [jnp.stack(a, axis=0) for a in outs_p]
    fk_s, fv_s, fl_s, sg_s, sc_s = [jnp.stack(a, axis=0) for a in outs_s]
    return (yp, ys, fk_p, fv_p, fl_p, sg_p, sc_p, fk_s, fv_s, fl_s, sg_s, sc_s)
```

```python
import functools

import jax
import jax.numpy as jnp
from jax import lax
from jax.experimental import pallas as pl
from jax.experimental.pallas import tpu as pltpu

F32 = jnp.float32
BF16 = jnp.bfloat16
LANE = 128
LN_EPS = 1e-5
RMS_EPS = 1e-6
L2_EPS = 1e-6
CONV_PAD = 8
VMEM_LIMIT = 58 * 1024 * 1024
HIGHEST = lax.Precision.HIGHEST
NEG_INF = float("-inf")


def _params(sem):
    return pltpu.CompilerParams(dimension_semantics=sem, vmem_limit_bytes=VMEM_LIMIT)


def _tile(n, pref, align=LANE):
    if n <= pref:
        return n
    t = (pref // align) * align
    while t >= align:
        if n % t == 0:
            return t
        t -= align
    return n


def _dot(a, b):
    return jnp.dot(a, b, preferred_element_type=F32)


def _dot_nt(a, b):
    return lax.dot_general(a, b, (((1,), (1,)), ((), ())), preferred_element_type=F32)


def _dot_tn(a, b):
    return lax.dot_general(a, b, (((0,), (0,)), ((), ())), preferred_element_type=F32)


def _sigmoid(x):
    return 1.0 / (1.0 + jnp.exp(-x))


def _softplus(x):
    return jnp.maximum(x, 0.0) + jnp.log1p(jnp.exp(-jnp.abs(x)))


def _silu(x):
    return x * _sigmoid(x)


def _inproj_kernel(x_ref, w_ref, ws_ref, o_ref, s_ref, st_ref, xb_ref):
    @pl.when(pl.program_id(1) == 0)
    def _():
        xb = x_ref[...].astype(BF16)
        xb_ref[...] = xb
        sm = _dot(xb, ws_ref[...])
        s_ref[...] = sm
        st_ref[...] = sm.T

    o_ref[...] = _dot(xb_ref[...], w_ref[...])


def _inproj(x2d, w_main, w_small):
    m, d = x2d.shape
    n = w_main.shape[1]
    tm = _tile(m, 512)
    tn = _tile(n, 1024)
    return pl.pallas_call(
        _inproj_kernel,
        grid=(m // tm, n // tn),
        in_specs=[
            pl.BlockSpec((tm, d), lambda i, j: (i, 0)),
            pl.BlockSpec((d, tn), lambda i, j: (0, j)),
            pl.BlockSpec((d, LANE), lambda i, j: (0, 0)),
        ],
        out_specs=[
            pl.BlockSpec((tm, tn), lambda i, j: (i, j)),
            pl.BlockSpec((tm, LANE), lambda i, j: (i, 0)),
            pl.BlockSpec((LANE, tm), lambda i, j: (0, i)),
        ],
        out_shape=[
            jax.ShapeDtypeStruct((m, n), F32),
            jax.ShapeDtypeStruct((m, LANE), F32),
            jax.ShapeDtypeStruct((LANE, m), F32),
        ],
        scratch_shapes=[pltpu.VMEM((tm, d), BF16)],
        compiler_params=_params(("parallel", "arbitrary")),
        name="inproj",
    )(x2d, w_main, w_small)


def _gate_values(z, a_log, bias, idx, ah, bh):
    zz = z + bias
    g = -jnp.exp(a_log) * _softplus(zz)
    beta = _sigmoid(zz)
    logf = -_softplus(-zz)
    return jnp.where(idx < ah, g, jnp.where(idx < 2 * ah, beta, jnp.where(idx < 2 * ah + 2 * bh, logf, 0.0)))


def _gate_merge(idx, cs, y, carry, ah, bh):
    return jnp.where(idx < ah, cs,
                     jnp.where(idx < 2 * ah, y,
                               jnp.where(idx < 2 * ah + bh, cs + carry,
                                         jnp.where(idx < 2 * ah + 2 * bh, y, 0.0))))


def _gates_kernel(apply, c_len, ah, bh, sm_ref, smt_ref, cc_ref, cr_ref, prow_ref, pcol_ref,
                  col_ref, rowc_ref, rowf_ref):
    t_len = sm_ref.shape[1]
    nc = t_len // c_len
    ii = lax.broadcasted_iota(jnp.int32, (c_len, c_len), 0)
    jj = lax.broadcasted_iota(jnp.int32, (c_len, c_len), 1)
    tril = (ii >= jj).astype(F32)
    triu = (ii <= jj).astype(F32)
    lane = lax.broadcasted_iota(jnp.int32, (c_len, LANE), 1)
    subl = lax.broadcasted_iota(jnp.int32, (LANE, c_len), 0)
    carry_c = cc_ref[0]
    carry_r = cr_ref[0]
    for c in range(nc):
        z = sm_ref[0, c * c_len:(c + 1) * c_len, :]
        y = _gate_values(z, prow_ref[0:1, :], prow_ref[1:2, :], lane, ah, bh) if apply else z
        cs = jnp.dot(tril, y, precision=HIGHEST, preferred_element_type=F32)
        col_ref[0, c * c_len:(c + 1) * c_len, :] = _gate_merge(lane, cs, y, carry_c, ah, bh)
        carry_c = carry_c + cs[c_len - 1:c_len, :]

        zt = smt_ref[0, :, c * c_len:(c + 1) * c_len]
        yt = _gate_values(zt, pcol_ref[:, 0:1], pcol_ref[:, 1:2], subl, ah, bh) if apply else zt
        cst = jnp.dot(yt, triu, precision=HIGHEST, preferred_element_type=F32)
        out_t = _gate_merge(subl, cst, yt, carry_r, ah, bh)
        rowc_ref[0, c] = out_t
        rowf_ref[0, :, c * c_len:(c + 1) * c_len] = out_t
        carry_r = carry_r + cst[:, c_len - 1:c_len]


def _gates(sm3, smt, carry_col, carry_row, prow, pcol, *, apply, c_len, ah, bh):
    b, t_len, _ = sm3.shape
    nc = t_len // c_len
    return pl.pallas_call(
        functools.partial(_gates_kernel, apply, c_len, ah, bh),
        grid=(b,),
        in_specs=[
            pl.BlockSpec((1, t_len, LANE), lambda i: (i, 0, 0)),
            pl.BlockSpec((1, LANE, t_len), lambda i: (i, 0, 0)),
            pl.BlockSpec((1, 1, LANE), lambda i: (i, 0, 0)),
            pl.BlockSpec((1, LANE, 1), lambda i: (i, 0, 0)),
            pl.BlockSpec((2, LANE), lambda i: (0, 0)),
            pl.BlockSpec((LANE, 2), lambda i: (0, 0)),
        ],
        out_specs=[
            pl.BlockSpec((1, t_len, LANE), lambda i: (i, 0, 0)),
            pl.BlockSpec((1, nc, LANE, c_len), lambda i: (i, 0, 0, 0)),
            pl.BlockSpec((1, LANE, t_len), lambda i: (i, 0, 0)),
        ],
        out_shape=[
            jax.ShapeDtypeStruct((b, t_len, LANE), F32),
            jax.ShapeDtypeStruct((b, nc, LANE, c_len), F32),
            jax.ShapeDtypeStruct((b, LANE, t_len), F32),
        ],
        compiler_params=_params(("parallel",)),
        name="gates",
    )(sm3, smt, carry_col, carry_row, prow, pcol)


def _unit_lower_inverse(low, eye, ii, jj):
    c_len = low.shape[0]
    same0 = (ii >> 1) == (jj >> 1)
    t_inv = eye - jnp.where(same0, low, 0.0)
    k = 1
    while (2 << k) <= c_len:
        sel = ((ii >> (k + 1)) == (jj >> (k + 1))) & (((ii >> k) & 1) == 1) & (((jj >> k) & 1) == 0)
        off = jnp.where(sel, low, 0.0).astype(BF16)
        tb = t_inv.astype(BF16)
        t_inv = t_inv - _dot(_dot(tb, off).astype(BF16), tb)
        k += 1
    resid = eye - jnp.dot(eye + low, t_inv, precision=HIGHEST, preferred_element_type=F32)
    return t_inv + _dot(t_inv.astype(BF16), resid.astype(BF16))


def _gdn_kernel(c_len, hb, ah, q_ref, k_ref, v_ref, z_ref, hq_ref, hk_ref, hv_ref,
                cq_ref, ck_ref, cv_ref, col_ref, rowc_ref, s0_ref, nw_ref,
                o_ref, sn_ref, s_scr, buf_scr):
    g_idx = pl.program_id(1)
    c_idx = pl.program_id(2)
    hd = nw_ref.shape[1]
    lo = CONV_PAD - 3

    @pl.when(c_idx == 0)
    def _():
        s_scr[...] = s0_ref[0]
        buf_scr[0, lo:CONV_PAD, :] = hq_ref[0]
        buf_scr[1, lo:CONV_PAD, :] = hk_ref[0]
        buf_scr[2, lo:CONV_PAD, :] = hv_ref[0]

    conv = []
    for n, (x_ref, w_ref) in enumerate(((q_ref, cq_ref), (k_ref, ck_ref), (v_ref, cv_ref))):
        buf_scr[n, CONV_PAD:CONV_PAD + c_len, :] = x_ref[0]
        acc = w_ref[0:1, :] * buf_scr[n, lo:lo + c_len, :]
        for w in range(1, 4):
            acc = acc + w_ref[w:w + 1, :] * buf_scr[n, lo + w:lo + w + c_len, :]
        buf_scr[n, lo:CONV_PAD, :] = buf_scr[n, lo + c_len:CONV_PAD + c_len, :]
        conv.append(_silu(acc))
    qc, kc, vc = conv

    colblk = col_ref[0]
    lane = lax.broadcasted_iota(jnp.int32, colblk.shape, 1)
    ii = lax.broadcasted_iota(jnp.int32, (c_len, c_len), 0)
    jj = lax.broadcasted_iota(jnp.int32, (c_len, c_len), 1)
    eye = (ii == jj).astype(F32)
    nw = nw_ref[...]

    for hh in range(hb):
        head = g_idx * hb + hh
        sl = slice(hh * hd, (hh + 1) * hd)
        cum_c = jnp.sum(jnp.where(lane == head, colblk, 0.0), axis=1, keepdims=True)
        beta_c = jnp.sum(jnp.where(lane == ah + head, colblk, 0.0), axis=1, keepdims=True)
        cum_r = rowc_ref[0, 0, pl.ds(head, 1), :]
        cum_last = cum_r[:, c_len - 1:c_len]

        q = qc[:, sl]
        k = kc[:, sl]
        v = vc[:, sl]
        q = q * lax.rsqrt(jnp.sum(q * q, axis=1, keepdims=True) + L2_EPS) * (hd ** -0.5)
        k = k * lax.rsqrt(jnp.sum(k * k, axis=1, keepdims=True) + L2_EPS)
        qb = q.astype(BF16)
        kb = k.astype(BF16)

        decay = jnp.exp(jnp.where(ii >= jj, cum_c - cum_r, NEG_INF))
        kk = _dot_nt(kb, kb)
        low = jnp.where(ii > jj, beta_c * kk * decay, 0.0)
        t_inv = _unit_lower_inverse(low, eye, ii, jj)

        e_cum = jnp.exp(cum_c)
        rhs = jnp.concatenate([beta_c * v, (beta_c * e_cum) * k], axis=1).astype(BF16)
        w_all = _dot(t_inv.astype(BF16), rhs)
        w_v = w_all[:, :hd]
        w_k = w_all[:, hd:]

        s_old = s_scr[hh]
        sb = s_old.astype(BF16)
        u = w_v - _dot(w_k.astype(BF16), sb)
        ub = u.astype(BF16)
        qk = _dot_nt(qb, kb) * decay
        o = _dot((q * e_cum).astype(BF16), sb) + _dot(qk.astype(BF16), ub)
        k_dec = k * jnp.exp(cum_last - cum_c)
        s_scr[hh] = s_old * jnp.exp(cum_last) + _dot_tn(k_dec.astype(BF16), ub)

        o = o * lax.rsqrt(jnp.mean(o * o, axis=1, keepdims=True) + RMS_EPS) * nw
        o_ref[0, :, sl] = (o * _silu(z_ref[0, :, sl])).astype(o_ref.dtype)

    @pl.when(c_idx == pl.num_programs(2) - 1)
    def _():
        sn_ref[0] = s_scr[...]


def _gdn(proj3, conv_hist, conv_w, col, rowc, s0, norm_w, *, c_len, ah, hd):
    b, t_len, _ = proj3.shape
    nc = t_len // c_len
    aw = ah * hd
    hb = ah if ah <= 8 else 8
    ng = ah // hb
    wb = hb * hd
    kq, kk_, kv, kz = 0, ng, 2 * ng, 3 * ng
    tok = lambda off: pl.BlockSpec((1, c_len, wb), lambda i, g, c: (i, c, off + g))
    hist = lambda off: pl.BlockSpec((1, 3, wb), lambda i, g, c: (i, 0, off + g))
    cw = lambda off: pl.BlockSpec((4, wb), lambda i, g, c: (0, off + g))
    return pl.pallas_call(
        functools.partial(_gdn_kernel, c_len, hb, ah),
        grid=(b, ng, nc),
        in_specs=[
            tok(kq), tok(kk_), tok(kv), tok(kz),
            hist(kq), hist(kk_), hist(kv),
            cw(kq), cw(kk_), cw(kv),
            pl.BlockSpec((1, c_len, LANE), lambda i, g, c: (i, c, 0)),
            pl.BlockSpec((1, 1, LANE, c_len), lambda i, g, c: (i, c, 0, 0)),
            pl.BlockSpec((1, hb, hd, hd), lambda i, g, c: (i, g, 0, 0)),
            pl.BlockSpec((1, hd), lambda i, g, c: (0, 0)),
        ],
        out_specs=[
            pl.BlockSpec((1, c_len, wb), lambda i, g, c: (i, c, g)),
            pl.BlockSpec((1, hb, hd, hd), lambda i, g, c: (i, g, 0, 0)),
        ],
        out_shape=[
            jax.ShapeDtypeStruct((b, t_len, aw), BF16),
            jax.ShapeDtypeStruct((b, ah, hd, hd), F32),
        ],
        scratch_shapes=[
            pltpu.VMEM((hb, hd, hd), F32),
            pltpu.VMEM((3, CONV_PAD + c_len, wb), F32),
        ],
        compiler_params=_params(("parallel", "parallel", "arbitrary")),
        name="gdn",
    )(proj3, proj3, proj3, proj3, conv_hist, conv_hist, conv_hist, conv_w, conv_w, conv_w,
      col, rowc, s0, norm_w)


def _fox_kernel(scale, q_off, f_lane, q_ref, k_ref, v_ref, cq_ref, ck_ref, o_ref, m_scr, l_scr, acc_scr):
    h = pl.program_id(1)
    qi = pl.program_id(2)
    kj = pl.program_id(3)
    tq = q_ref.shape[1]
    tk = k_ref.shape[1]

    @pl.when(kj == 0)
    def _():
        m_scr[...] = jnp.full(m_scr.shape, NEG_INF, F32)
        l_scr[...] = jnp.zeros(l_scr.shape, F32)
        acc_scr[...] = jnp.zeros(acc_scr.shape, F32)

    @pl.when(kj * tk <= qi * tq + (tq - 1) + q_off)
    def _():
        s = _dot_nt(q_ref[0].astype(BF16), k_ref[0].astype(BF16)) * scale
        cqb = cq_ref[0]
        lane = lax.broadcasted_iota(jnp.int32, cqb.shape, 1)
        cq = jnp.sum(jnp.where(lane == f_lane + h, cqb, 0.0), axis=1, keepdims=True)
        ck = ck_ref[0, pl.ds(f_lane + h, 1), :]
        qpos = qi * tq + q_off + lax.broadcasted_iota(jnp.int32, (tq, tk), 0)
        kpos = kj * tk + lax.broadcasted_iota(jnp.int32, (tq, tk), 1)
        s = jnp.where(kpos <= qpos, s + (cq - ck), NEG_INF)
        m_old = m_scr[...]
        m_new = jnp.maximum(m_old, jnp.max(s, axis=1, keepdims=True))
        p = jnp.exp(s - m_new)
        alpha = jnp.exp(m_old - m_new)
        l_scr[...] = alpha * l_scr[...] + jnp.sum(p, axis=1, keepdims=True)
        acc_scr[...] = alpha * acc_scr[...] + _dot(p.astype(BF16), v_ref[0].astype(BF16))
        m_scr[...] = m_new

    @pl.when(kj == pl.num_programs(3) - 1)
    def _():
        o_ref[0] = (acc_scr[...] / l_scr[...]).astype(o_ref.dtype)


def _fox(q_arr, q_blk, k_arr, k_blk, v_arr, v_blk, cq_col, ck_row, *, bh, hd, f_lane):
    b, t_q, _ = q_arr.shape
    t_k = k_arr.shape[1]
    q_off = t_k - t_q
    tq = _tile(t_q, 512, 8)
    tk = _tile(t_k, 512)
    last = lambda qi: (qi * tq + (tq - 1) + q_off) // tk
    return pl.pallas_call(
        functools.partial(_fox_kernel, hd ** -0.5, q_off, f_lane),
        grid=(b, bh, t_q // tq, t_k // tk),
        in_specs=[
            pl.BlockSpec((1, tq, hd), lambda i, h, qi, kj: (i, qi, q_blk + h)),
            pl.BlockSpec((1, tk, hd), lambda i, h, qi, kj: (i, jnp.minimum(kj, last(qi)), k_blk + h)),
            pl.BlockSpec((1, tk, hd), lambda i, h, qi, kj: (i, jnp.minimum(kj, last(qi)), v_blk + h)),
            pl.BlockSpec((1, tq, LANE), lambda i, h, qi, kj: (i, qi, 0)),
            pl.BlockSpec((1, LANE, tk), lambda i, h, qi, kj: (i, 0, jnp.minimum(kj, last(qi)))),
        ],
        out_specs=pl.BlockSpec((1, tq, hd), lambda i, h, qi, kj: (i, qi, h)),
        out_shape=jax.ShapeDtypeStruct((b, t_q, bh * hd), BF16),
        scratch_shapes=[
            pltpu.VMEM((tq, 1), F32),
            pltpu.VMEM((tq, 1), F32),
            pltpu.VMEM((tq, hd), F32),
        ],
        compiler_params=_params(("parallel", "parallel", "parallel", "arbitrary")),
        name="fox",
    )(q_arr, k_arr, v_arr, cq_col, ck_row)


def _layer_norm_rows(x, g, b):
    mu = jnp.mean(x, axis=1, keepdims=True)
    xc = x - mu
    var = jnp.mean(xc * xc, axis=1, keepdims=True)
    return xc * lax.rsqrt(var + LN_EPS) * g + b


def _layer_norm_ref(ref, g_ref, b_ref, rows):
    def body(r, carry):
        sl = pl.ds(pl.multiple_of(r * rows, rows), rows)
        ref[sl, :] = _layer_norm_rows(ref[sl, :], g_ref[...], b_ref[...])
        return carry
    lax.fori_loop(0, ref.shape[0] // rows, body, 0)


def _outproj_kernel(alpha, tn, oa_ref, ob_ref, wa_ref, wb_ref, x_ref, g_ref, b_ref, hid_ref, hidt_ref):
    j = pl.program_id(1)
    col = pl.multiple_of(j * tn, LANE)
    hid_ref[:, pl.ds(col, tn)] = (alpha * x_ref[...] + _dot(oa_ref[...], wa_ref[...])
                                  + _dot(ob_ref[...], wb_ref[...]))

    @pl.when(j == pl.num_programs(1) - 1)
    def _():
        tm, d = hid_ref.shape
        _layer_norm_ref(hid_ref, g_ref, b_ref, min(tm, 32))
        rb = min(tm, LANE)
        step = _tile(d, 512)
        for r in range(tm // rb):
            for c in range(d // step):
                hidt_ref[c * step:(c + 1) * step, r * rb:(r + 1) * rb] = (
                    hid_ref[r * rb:(r + 1) * rb, c * step:(c + 1) * step].T.astype(BF16))


def _outproj(o_a, o_b, w_a, w_b, x2d, g, b, alpha):
    m, d = x2d.shape
    tm = _tile(m, 512)
    tn = _tile(d, 512)
    return pl.pallas_call(
        functools.partial(_outproj_kernel, alpha, tn),
        grid=(m // tm, d // tn),
        in_specs=[
            pl.BlockSpec((tm, o_a.shape[1]), lambda i, j: (i, 0)),
            pl.BlockSpec((tm, o_b.shape[1]), lambda i, j: (i, 0)),
            pl.BlockSpec((w_a.shape[0], tn), lambda i, j: (0, j)),
            pl.BlockSpec((w_b.shape[0], tn), lambda i, j: (0, j)),
            pl.BlockSpec((tm, tn), lambda i, j: (i, j)),
            pl.BlockSpec((1, d), lambda i, j: (0, 0)),
            pl.BlockSpec((1, d), lambda i, j: (0, 0)),
        ],
        out_specs=[
            pl.BlockSpec((tm, d), lambda i, j: (i, 0)),
            pl.BlockSpec((d, tm), lambda i, j: (0, i)),
        ],
        out_shape=[
            jax.ShapeDtypeStruct((m, d), F32),
            jax.ShapeDtypeStruct((d, m), BF16),
        ],
        compiler_params=_params(("parallel", "arbitrary")),
        name="outproj",
    )(o_a, o_b, w_a, w_b, x2d, g, b)


def _top_values(x, n):
    vals = []
    for _ in range(n):
        m = jnp.max(x, axis=0, keepdims=True)
        vals.append(m)
        x = jnp.where(x == m, NEG_INF, x)
    return vals


def _route_kernel(topk, ht_ref, wq_ref, key_ref, s1_ref, s2_ref, a1_ref, a2_ref, tau_ref, cand_scr):
    dk = key_ref.shape[3]
    qt = _dot(wq_ref[...], ht_ref[...])
    s1 = _dot(key_ref[0, 0], qt[:dk].astype(BF16))
    s2 = _dot(key_ref[0, 1], qt[dk:].astype(BF16))
    top1 = _top_values(s1, topk)
    top2 = _top_values(s2, topk)
    cand_scr[...] = jnp.full(cand_scr.shape, NEG_INF, F32)
    pairs = [(a, b) for a in range(topk) for b in range(topk) if (a + 1) * (b + 1) <= topk]
    for r, (a, b) in enumerate(pairs):
        cand_scr[r:r + 1, :] = top1[a] + top2[b]
    best = _top_values(cand_scr[...], topk)
    z = jnp.exp(best[0] - best[0])
    for t in best[1:]:
        z = z + jnp.exp(t - best[0])
    s1_ref[0] = s1
    s2_ref[0] = s2
    a1_ref[0] = jnp.exp(s1 - top1[0]) / z
    a2_ref[0] = jnp.exp(s2 - top2[0])
    tau_ref[0] = jnp.broadcast_to(best[topk - 1], tau_ref.shape[1:])


def _num_candidates(topk):
    n = sum(topk // (a + 1) for a in range(topk))
    return -(-n // 8) * 8


def _route(hid_t, wq_t, keys, topk):
    d, m = hid_t.shape
    ph, _, nk, dk = keys.shape
    tm = _tile(m, 512)
    out = jax.ShapeDtypeStruct((ph, nk, m), F32)
    blk = pl.BlockSpec((1, nk, tm), lambda i, h: (h, 0, i))
    return pl.pallas_call(
        functools.partial(_route_kernel, topk),
        grid=(m // tm, ph),
        in_specs=[
            pl.BlockSpec((d, tm), lambda i, h: (0, i)),
            pl.BlockSpec((2 * dk, d), lambda i, h: (h, 0)),
            pl.BlockSpec((1, 2, nk, dk), lambda i, h: (h, 0, 0, 0)),
        ],
        out_specs=[blk, blk, blk, blk, pl.BlockSpec((1, 8, tm), lambda i, h: (h, 0, i))],
        out_shape=[out, out, out, out, jax.ShapeDtypeStruct((ph, 8, m), F32)],
        scratch_shapes=[pltpu.VMEM((_num_candidates(topk), tm), F32)],
        compiler_params=_params(("parallel", "arbitrary")),
        name="peer_route",
    )(hid_t, wq_t, keys)


def _peer_kernel(ht_ref, u_ref, vt_ref, s1_ref, s2_ref, a1_ref, a2_ref, tau_ref, o_ref, act_scr):
    e = pl.program_id(1)
    ph, nk, _ = s2_ref.shape
    te = u_ref.shape[0]
    n_sub = te // nk

    @pl.when(e == 0)
    def _():
        o_ref[...] = jnp.zeros(o_ref.shape, F32)

    pre = _dot(u_ref[...], ht_ref[...])
    for sub in range(n_sub):
        row = e * n_sub + sub
        gate = None
        for h in range(ph):
            s1r = s1_ref[h, pl.ds(row, 1), :]
            a1r = a1_ref[h, pl.ds(row, 1), :]
            hit = (s2_ref[h] + s1r) >= tau_ref[h, 0:1, :]
            term = jnp.where(hit, a2_ref[h], 0.0) * a1r
            gate = term if gate is None else gate + term
        x = pre[sub * nk:(sub + 1) * nk]
        act = gate * (0.5 * x * (1.0 + lax.erf(x * (2.0 ** -0.5))))
        act_scr[sub * nk:(sub + 1) * nk, :] = act.astype(BF16)
    d = o_ref.shape[0]
    step = _tile(d, 512)

    def body(c, carry):
        sl = pl.ds(pl.multiple_of(c * step, step), step)
        o_ref[sl, :] += _dot(vt_ref[sl, :], act_scr[...])
        return carry
    lax.fori_loop(0, d // step, body, 0)


def _peer(hid_t, u_b, v_t, s1, s2, a1, a2, tau):
    d, m = hid_t.shape
    ne = u_b.shape[0]
    ph, nk, _ = s1.shape
    tm = _tile(m, 512)
    te = _tile(ne, 512)
    rt = pl.BlockSpec((ph, nk, tm), lambda i, e: (0, 0, i))
    return pl.pallas_call(
        _peer_kernel,
        grid=(m // tm, ne // te),
        in_specs=[
            pl.BlockSpec((d, tm), lambda i, e: (0, i)),
            pl.BlockSpec((te, d), lambda i, e: (e, 0)),
            pl.BlockSpec((d, te), lambda i, e: (0, e)),
            rt, rt, rt, rt,
            pl.BlockSpec((ph, 8, tm), lambda i, e: (0, 0, i)),
        ],
        out_specs=pl.BlockSpec((d, tm), lambda i, e: (0, i)),
        out_shape=jax.ShapeDtypeStruct((d, m), F32),
        scratch_shapes=[pltpu.VMEM((te, tm), BF16)],
        compiler_params=_params(("parallel", "arbitrary")),
        name="peer_dense",
    )(hid_t, u_b, v_t, s1, s2, a1, a2, tau)


def _final_kernel(alpha, hid_ref, pt_ref, g_ref, b_ref, y_ref):
    d = hid_ref.shape[1]
    step = _tile(d, 512)
    for c in range(d // step):
        sl = slice(c * step, (c + 1) * step)
        y_ref[:, sl] = alpha * hid_ref[:, sl] + pt_ref[sl, :].T
    _layer_norm_ref(y_ref, g_ref, b_ref, min(y_ref.shape[0], 32))


def _final(hid, peer_t, g, b, alpha):
    m, d = hid.shape
    tm = _tile(m, 256)
    return pl.pallas_call(
        functools.partial(_final_kernel, alpha),
        grid=(m // tm,),
        in_specs=[
            pl.BlockSpec((tm, d), lambda i: (i, 0)),
            pl.BlockSpec((d, tm), lambda i: (0, i)),
            pl.BlockSpec((1, d), lambda i: (0, 0)),
            pl.BlockSpec((1, d), lambda i: (0, 0)),
        ],
        out_specs=pl.BlockSpec((tm, d), lambda i: (i, 0)),
        out_shape=jax.ShapeDtypeStruct((m, d), F32),
        compiler_params=_params(("parallel",)),
        name="final_ln",
    )(hid, peer_t, g, b)


def _pad_lanes(a):
    return jnp.pad(a, [(0, 0)] * (a.ndim - 1) + [(0, LANE - a.shape[-1])])


def _trunk(x, conv_hist, s0, fox_cache, wts, depth):
    (w_main, w_small, conv_w, prow, pcol, norm_w, w_a, w_b, ln1_g, ln1_b, wq_t, keys, u_b, v_t,
     ln2_g, ln2_b, ah, bh, hd, topk) = wts
    b, t_len, d = x.shape
    m = b * t_len
    aw, bw = ah * hd, bh * hd
    alpha = (2 * depth) ** 0.25
    f_lane = 2 * ah
    x2d = x.reshape(m, d)

    proj, sm, smt = _inproj(x2d, w_main, w_small)
    proj3 = proj.reshape(b, t_len, -1)
    n_a = 4 * aw
    k_new = proj3[:, :, n_a + bw:n_a + 2 * bw]
    v_new = proj3[:, :, n_a + 2 * bw:n_a + 3 * bw]

    c_len = min(64, t_len)
    zero_c = jnp.zeros((b, 1, LANE), F32)
    zero_r = jnp.zeros((b, LANE, 1), F32)
    if fox_cache is None:
        carry_c, carry_r = zero_c, zero_r
    else:
        clf = fox_cache[2].astype(F32)
        p_len = clf.shape[1]
        clf_col = jnp.pad(clf, ((0, 0), (0, 0), (f_lane, LANE - f_lane - bh)))
        clf_row = jnp.swapaxes(clf_col, 1, 2)
        cc_col, _, cc_row = _gates(clf_col, clf_row, zero_c, zero_r, prow, pcol,
                                   apply=False, c_len=min(64, p_len), ah=ah, bh=bh)
        carry_c = cc_col[:, p_len - 1:, :]
        carry_r = cc_row[:, :, p_len - 1:]
    smt3 = jnp.swapaxes(smt.reshape(LANE, b, t_len), 0, 1)
    col, rowc, rowf = _gates(sm.reshape(b, t_len, LANE), smt3, carry_c, carry_r, prow, pcol,
                             apply=True, c_len=c_len, ah=ah, bh=bh)
    logf = col[:, :, f_lane + bh:f_lane + 2 * bh]

    o_a, s_new = _gdn(proj3, conv_hist, conv_w, col, rowc, s0, norm_w, c_len=c_len, ah=ah, hd=hd)
    conv_new = jnp.concatenate([conv_hist.astype(F32), proj3[:, :, :3 * aw]], axis=1)[:, -3:]

    qb0 = n_a // hd
    if fox_cache is None:
        o_b = _fox(proj3, qb0, proj3, qb0 + bh, proj3, qb0 + 2 * bh, col, rowf, bh=bh, hd=hd, f_lane=f_lane)
    else:
        k_all = jnp.concatenate([fox_cache[0].reshape(b, -1, bw).astype(F32), k_new], axis=1)
        v_all = jnp.concatenate([fox_cache[1].reshape(b, -1, bw).astype(F32), v_new], axis=1)
        ck_row = jnp.concatenate([cc_row, rowf], axis=2)
        o_b = _fox(proj3, qb0, k_all, 0, v_all, 0, col, ck_row, bh=bh, hd=hd, f_lane=f_lane)

    hid, hid_t = _outproj(o_a.reshape(m, aw), o_b.reshape(m, bw), w_a, w_b, x2d, ln1_g, ln1_b, alpha)
    s1, s2, a1, a2, tau = _route(hid_t, wq_t, keys, topk)
    peer_t = _peer(hid_t, u_b, v_t, s1, s2, a1, a2, tau)
    y = _final(hid, peer_t, ln2_g, ln2_b, alpha).reshape(b, t_len, d)
    return y, (k_new.reshape(b, t_len, bh, hd), v_new.reshape(b, t_len, bh, hd), logf, s_new, conv_new)


def kernel(x_prompt, x_sample, cache_fox_k, cache_fox_v, cache_fox_logf, state_gdn, state_gdn_conv,
           w_in, gdn_conv_w, gdn_a_log, gdn_dt_bias, gdn_norm_w, fox_f_bias, w_out, ln1_g, ln1_b,
           peer_w_q, peer_sub_keys, peer_u, peer_v, ln2_g, ln2_b):
    depth = w_in.shape[0]
    ah = gdn_a_log.shape[1]
    bh = fox_f_bias.shape[1]
    hd = gdn_norm_w.shape[1]
    aw, bw = ah * hd, bh * hd
    topk = 16
    assert 2 * ah + 2 * bh <= LANE
    n_p = x_prompt.shape[0]
    yp, ys = x_prompt, x_sample
    outs_p, outs_s = [], []
    for l in range(depth):
        o_a_a = 4 * aw
        o_b_qkv = o_a_a + 2 * ah
        o_b_f = o_b_qkv + 3 * bw
        wl = w_in[l]
        w_main = jnp.concatenate([wl[:, :o_a_a], wl[:, o_b_qkv:o_b_f]], axis=1).astype(BF16)
        w_f = wl[:, o_b_f:o_b_f + bh]
        w_small = _pad_lanes(jnp.concatenate([wl[:, o_a_a:o_b_qkv], w_f, w_f], axis=1)).astype(BF16)
        zeros_a = jnp.zeros((ah,), F32)
        prow = jnp.stack([
            _pad_lanes(gdn_a_log[l].astype(F32)),
            _pad_lanes(jnp.concatenate([gdn_dt_bias[l].astype(F32), zeros_a,
                                        fox_f_bias[l].astype(F32), fox_f_bias[l].astype(F32)])),
        ])
        wts = (w_main, w_small, gdn_conv_w[l].astype(F32), prow, prow.T,
               gdn_norm_w[l].reshape(1, hd).astype(F32),
               w_out[l][:aw].astype(BF16), w_out[l][aw:].astype(BF16),
               ln1_g[l].reshape(1, -1), ln1_b[l].reshape(1, -1),
               peer_w_q[l].T.astype(BF16), peer_sub_keys[l].astype(BF16),
               peer_u[l].astype(BF16), peer_v[l].T.astype(BF16),
               ln2_g[l].reshape(1, -1), ln2_b[l].reshape(1, -1), ah, bh, hd, topk)
        conv0 = jnp.zeros((n_p, 3, 3 * aw), yp.dtype)
        s0 = jnp.zeros((n_p, ah, hd, hd), yp.dtype)
        yp, st_p = _trunk(yp, conv0, s0, None, wts, depth)
        ys, st_s = _trunk(ys, state_gdn_conv[l], state_gdn[l],
                          (cache_fox_k[l], cache_fox_v[l], cache_fox_logf[l]), wts, depth)
        outs_p.append(st_p)
        outs_s.append(st_s)
    stack = lambda outs, n: jnp.stack([o[n] for o in outs], axis=0)
    return ((yp, ys) + tuple(stack(outs_p, n) for n in range(5))
            + tuple(stack(outs_s, n) for n in range(5)))
```

```python
import functools

import jax
import jax.numpy as jnp
from jax import lax
from jax.experimental import pallas as pl
from jax.experimental.pallas import tpu as pltpu

F32 = jnp.float32
BF16 = jnp.bfloat16
LANE = 128
LN_EPS = 1e-5
RMS_EPS = 1e-6
L2_EPS = 1e-6
CONV_PAD = 8
VMEM_LIMIT = 58 * 1024 * 1024
HIGHEST = lax.Precision.HIGHEST
NEG_INF = float("-inf")
LOG2E = 1.4426950408889634


def _params(sem):
    return pltpu.CompilerParams(dimension_semantics=sem, vmem_limit_bytes=VMEM_LIMIT)


def _tile(n, pref, align=LANE):
    if n <= pref:
        return n
    t = (pref // align) * align
    while t >= align:
        if n % t == 0:
            return t
        t -= align
    return n


def _dot(a, b):
    return jnp.dot(a, b, preferred_element_type=F32)


def _dot_nt(a, b):
    return lax.dot_general(a, b, (((1,), (1,)), ((), ())), preferred_element_type=F32)


def _dot_tn(a, b):
    return lax.dot_general(a, b, (((0,), (0,)), ((), ())), preferred_element_type=F32)


def _sigmoid(x):
    return 1.0 / (1.0 + jnp.exp(-x))


def _softplus(x):
    return jnp.maximum(x, 0.0) + jnp.log1p(jnp.exp(-jnp.abs(x)))


def _silu(x):
    return x * _sigmoid(x)


def _inproj_kernel(x_ref, w_ref, ws_ref, o_ref, s_ref, st_ref, xb_ref):
    @pl.when(pl.program_id(1) == 0)
    def _():
        xb = x_ref[...].astype(BF16)
        xb_ref[...] = xb
        sm = _dot(xb, ws_ref[...])
        s_ref[...] = sm
        st_ref[...] = sm.T

    o_ref[...] = _dot(xb_ref[...], w_ref[...])


def _inproj(x2d, w_main, w_small):
    m, d = x2d.shape
    n = w_main.shape[1]
    tm = _tile(m, 512)
    tn = _tile(n, 1024)
    return pl.pallas_call(
        _inproj_kernel,
        grid=(m // tm, n // tn),
        in_specs=[
            pl.BlockSpec((tm, d), lambda i, j: (i, 0)),
            pl.BlockSpec((d, tn), lambda i, j: (0, j)),
            pl.BlockSpec((d, LANE), lambda i, j: (0, 0)),
        ],
        out_specs=[
            pl.BlockSpec((tm, tn), lambda i, j: (i, j)),
            pl.BlockSpec((tm, LANE), lambda i, j: (i, 0)),
            pl.BlockSpec((LANE, tm), lambda i, j: (0, i)),
        ],
        out_shape=[
            jax.ShapeDtypeStruct((m, n), F32),
            jax.ShapeDtypeStruct((m, LANE), F32),
            jax.ShapeDtypeStruct((LANE, m), F32),
        ],
        scratch_shapes=[pltpu.VMEM((tm, d), BF16)],
        compiler_params=_params(("parallel", "arbitrary")),
        name="inproj",
    )(x2d, w_main, w_small)


def _gate_values(z, a_log, bias, idx, ah, bh):
    zz = z + bias
    g = -jnp.exp(a_log) * _softplus(zz)
    beta = _sigmoid(zz)
    logf = -_softplus(-zz)
    return jnp.where(idx < ah, g, jnp.where(idx < 2 * ah, beta, jnp.where(idx < 2 * ah + 2 * bh, logf, 0.0)))


def _gate_merge(idx, cs, y, carry, ah, bh):
    return jnp.where(idx < ah, cs,
                     jnp.where(idx < 2 * ah, y,
                               jnp.where(idx < 2 * ah + bh, cs + carry,
                                         jnp.where(idx < 2 * ah + 2 * bh, y, 0.0))))


def _gates_kernel(apply, c_len, ah, bh, sm_ref, smt_ref, cc_ref, cr_ref, prow_ref, pcol_ref,
                  col_ref, rowc_ref, rowf_ref):
    t_len = sm_ref.shape[1]
    nc = t_len // c_len
    ii = lax.broadcasted_iota(jnp.int32, (c_len, c_len), 0)
    jj = lax.broadcasted_iota(jnp.int32, (c_len, c_len), 1)
    tril = (ii >= jj).astype(F32)
    triu = (ii <= jj).astype(F32)
    lane = lax.broadcasted_iota(jnp.int32, (c_len, LANE), 1)
    subl = lax.broadcasted_iota(jnp.int32, (LANE, c_len), 0)
    carry_c = cc_ref[0]
    carry_r = cr_ref[0]
    for c in range(nc):
        z = sm_ref[0, c * c_len:(c + 1) * c_len, :]
        y = _gate_values(z, prow_ref[0:1, :], prow_ref[1:2, :], lane, ah, bh) if apply else z
        cs = jnp.dot(tril, y, precision=HIGHEST, preferred_element_type=F32)
        col_ref[0, c * c_len:(c + 1) * c_len, :] = _gate_merge(lane, cs, y, carry_c, ah, bh)
        carry_c = carry_c + cs[c_len - 1:c_len, :]

        zt = smt_ref[0, :, c * c_len:(c + 1) * c_len]
        yt = _gate_values(zt, pcol_ref[:, 0:1], pcol_ref[:, 1:2], subl, ah, bh) if apply else zt
        cst = jnp.dot(yt, triu, precision=HIGHEST, preferred_element_type=F32)
        out_t = _gate_merge(subl, cst, yt, carry_r, ah, bh)
        rowc_ref[0, c] = out_t
        rowf_ref[0, :, c * c_len:(c + 1) * c_len] = out_t
        carry_r = carry_r + cst[:, c_len - 1:c_len]


def _gates(sm3, smt, carry_col, carry_row, prow, pcol, *, apply, c_len, ah, bh):
    b, t_len, _ = sm3.shape
    nc = t_len // c_len
    return pl.pallas_call(
        functools.partial(_gates_kernel, apply, c_len, ah, bh),
        grid=(b,),
        in_specs=[
            pl.BlockSpec((1, t_len, LANE), lambda i: (i, 0, 0)),
            pl.BlockSpec((1, LANE, t_len), lambda i: (i, 0, 0)),
            pl.BlockSpec((1, 1, LANE), lambda i: (i, 0, 0)),
            pl.BlockSpec((1, LANE, 1), lambda i: (i, 0, 0)),
            pl.BlockSpec((2, LANE), lambda i: (0, 0)),
            pl.BlockSpec((LANE, 2), lambda i: (0, 0)),
        ],
        out_specs=[
            pl.BlockSpec((1, t_len, LANE), lambda i: (i, 0, 0)),
            pl.BlockSpec((1, nc, LANE, c_len), lambda i: (i, 0, 0, 0)),
            pl.BlockSpec((1, LANE, t_len), lambda i: (i, 0, 0)),
        ],
        out_shape=[
            jax.ShapeDtypeStruct((b, t_len, LANE), F32),
            jax.ShapeDtypeStruct((b, nc, LANE, c_len), F32),
            jax.ShapeDtypeStruct((b, LANE, t_len), F32),
        ],
        compiler_params=_params(("parallel",)),
        name="gates",
    )(sm3, smt, carry_col, carry_row, prow, pcol)


def _unit_lower_inverses(lows, eye, ii, jj):
    c_len = lows[0].shape[0]
    same0 = (ii >> 1) == (jj >> 1)
    ts = [eye - jnp.where(same0, low, 0.0) for low in lows]
    k = 1
    while (2 << k) <= c_len:
        sel = ((ii >> (k + 1)) == (jj >> (k + 1))) & (((ii >> k) & 1) == 1) & (((jj >> k) & 1) == 0)
        tbs = [t.astype(BF16) for t in ts]
        mid = [_dot(tb, jnp.where(sel, low, 0.0).astype(BF16)).astype(BF16) for tb, low in zip(tbs, lows)]
        ts = [t - _dot(m, tb) for t, m, tb in zip(ts, mid, tbs)]
        k += 1
    resid = [(eye - jnp.dot(eye + low, t, precision=HIGHEST, preferred_element_type=F32)).astype(BF16)
             for low, t in zip(lows, ts)]
    return [t + _dot(t.astype(BF16), r) for t, r in zip(ts, resid)]


def _gdn_kernel(c_len, hb, ah, q_ref, k_ref, v_ref, z_ref, hq_ref, hk_ref, hv_ref,
                cq_ref, ck_ref, cv_ref, col_ref, rowc_ref, s0_ref, nw_ref,
                o_ref, sn_ref, s_scr, buf_scr):
    c_idx = pl.program_id(2)
    hd = nw_ref.shape[1]
    lo = CONV_PAD - 3

    @pl.when(c_idx == 0)
    def _():
        s_scr[...] = s0_ref[0]
        buf_scr[0, lo:CONV_PAD, :] = hq_ref[0]
        buf_scr[1, lo:CONV_PAD, :] = hk_ref[0]
        buf_scr[2, lo:CONV_PAD, :] = hv_ref[0]

    conv = []
    for n, (x_ref, w_ref) in enumerate(((q_ref, cq_ref), (k_ref, ck_ref), (v_ref, cv_ref))):
        buf_scr[n, CONV_PAD:CONV_PAD + c_len, :] = x_ref[0]
        acc = w_ref[0:1, :] * buf_scr[n, lo:lo + c_len, :]
        for w in range(1, 4):
            acc = acc + w_ref[w:w + 1, :] * buf_scr[n, lo + w:lo + w + c_len, :]
        buf_scr[n, lo:CONV_PAD, :] = buf_scr[n, lo + c_len:CONV_PAD + c_len, :]
        conv.append(_silu(acc))
    qc, kc, vc = conv

    colblk = col_ref[0]
    ii = lax.broadcasted_iota(jnp.int32, (c_len, c_len), 0)
    jj = lax.broadcasted_iota(jnp.int32, (c_len, c_len), 1)
    eye = (ii == jj).astype(F32)
    nw = nw_ref[...]
    heads = range(hb)
    sls = [slice(h * hd, (h + 1) * hd) for h in heads]

    cum_c = [colblk[:, h:h + 1] for h in heads]
    beta_c = [colblk[:, ah + h:ah + h + 1] for h in heads]
    cum_r = [rowc_ref[0, 0, h:h + 1, :] for h in heads]
    cum_last = [r[:, c_len - 1:c_len] for r in cum_r]
    e_cum = [jnp.exp(c) for c in cum_c]

    q = [qc[:, sl] for sl in sls]
    k = [kc[:, sl] for sl in sls]
    q = [x * (lax.rsqrt(jnp.sum(x * x, axis=1, keepdims=True) + L2_EPS) * (hd ** -0.5)) for x in q]
    k = [x * lax.rsqrt(jnp.sum(x * x, axis=1, keepdims=True) + L2_EPS) for x in k]
    qb = [x.astype(BF16) for x in q]
    kb = [x.astype(BF16) for x in k]

    decay = [jnp.exp(jnp.where(ii >= jj, c - r, NEG_INF)) for c, r in zip(cum_c, cum_r)]
    kk = [_dot_nt(x, x) for x in kb]
    qk = [_dot_nt(x, y) for x, y in zip(qb, kb)]
    lows = [jnp.where(ii > jj, b * m * d, 0.0) for b, m, d in zip(beta_c, kk, decay)]
    t_inv = _unit_lower_inverses(lows, eye, ii, jj)

    rhs = [jnp.concatenate([b * vc[:, sl], (b * e) * x], axis=1).astype(BF16)
           for b, e, x, sl in zip(beta_c, e_cum, k, sls)]
    w_all = [_dot(t.astype(BF16), r) for t, r in zip(t_inv, rhs)]

    s_old = [s_scr[h] for h in heads]
    sb = [s.astype(BF16) for s in s_old]
    u = [w[:, :hd] - _dot(w[:, hd:].astype(BF16), s) for w, s in zip(w_all, sb)]
    ub = [x.astype(BF16) for x in u]
    o = [_dot((x * e).astype(BF16), s) + _dot((m * d).astype(BF16), y)
         for x, e, s, m, d, y in zip(q, e_cum, sb, qk, decay, ub)]
    for h in heads:
        k_dec = k[h] * jnp.exp(cum_last[h] - cum_c[h])
        s_scr[h] = s_old[h] * jnp.exp(cum_last[h]) + _dot_tn(k_dec.astype(BF16), ub[h])
    for h in heads:
        x = o[h] * lax.rsqrt(jnp.mean(o[h] * o[h], axis=1, keepdims=True) + RMS_EPS) * nw
        o_ref[0, :, sls[h]] = (x * _silu(z_ref[0, :, sls[h]])).astype(o_ref.dtype)

    @pl.when(c_idx == pl.num_programs(2) - 1)
    def _():
        sn_ref[0] = s_scr[...]


def _gdn(proj3, conv_hist, conv_w, col, rowc, s0, norm_w, *, c_len, ah, hd):
    b, t_len, _ = proj3.shape
    nc = t_len // c_len
    aw = ah * hd
    hb = ah
    ng = ah // hb
    wb = hb * hd
    kq, kk_, kv, kz = 0, ng, 2 * ng, 3 * ng
    tok = lambda off: pl.BlockSpec((1, c_len, wb), lambda i, g, c: (i, c, off + g))
    hist = lambda off: pl.BlockSpec((1, 3, wb), lambda i, g, c: (i, 0, off + g))
    cw = lambda off: pl.BlockSpec((4, wb), lambda i, g, c: (0, off + g))
    return pl.pallas_call(
        functools.partial(_gdn_kernel, c_len, hb, ah),
        grid=(b, ng, nc),
        in_specs=[
            tok(kq), tok(kk_), tok(kv), tok(kz),
            hist(kq), hist(kk_), hist(kv),
            cw(kq), cw(kk_), cw(kv),
            pl.BlockSpec((1, c_len, LANE), lambda i, g, c: (i, c, 0)),
            pl.BlockSpec((1, 1, LANE, c_len), lambda i, g, c: (i, c, 0, 0)),
            pl.BlockSpec((1, hb, hd, hd), lambda i, g, c: (i, g, 0, 0)),
            pl.BlockSpec((1, hd), lambda i, g, c: (0, 0)),
        ],
        out_specs=[
            pl.BlockSpec((1, c_len, wb), lambda i, g, c: (i, c, g)),
            pl.BlockSpec((1, hb, hd, hd), lambda i, g, c: (i, g, 0, 0)),
        ],
        out_shape=[
            jax.ShapeDtypeStruct((b, t_len, aw), BF16),
            jax.ShapeDtypeStruct((b, ah, hd, hd), F32),
        ],
        scratch_shapes=[
            pltpu.VMEM((hb, hd, hd), F32),
            pltpu.VMEM((3, CONV_PAD + c_len, wb), F32),
        ],
        compiler_params=_params(("parallel", "parallel", "arbitrary")),
        name="gdn",
    )(proj3, proj3, proj3, proj3, conv_hist, conv_hist, conv_hist, conv_w, conv_w, conv_w,
      col, rowc, s0, norm_w)


def _fox_kernel(q_off, f_lane, hg, hd, q_ref, k_ref, v_ref, cq_ref, ck_ref, o_ref, m_scr, l_scr, acc_scr):
    g = pl.program_id(1)
    qi = pl.program_id(2)
    kj = pl.program_id(3)
    tq = q_ref.shape[1]
    tk = k_ref.shape[1]
    heads = range(hg)
    sls = [slice(h * hd, (h + 1) * hd) for h in heads]
    q_scale = (hd ** -0.5) * LOG2E

    @pl.when(kj == 0)
    def _():
        m_scr[...] = jnp.full(m_scr.shape, NEG_INF, F32)
        l_scr[...] = jnp.zeros(l_scr.shape, F32)
        acc_scr[...] = jnp.zeros(acc_scr.shape, F32)

    def update(masked):
        cqb = cq_ref[0] * LOG2E
        lane = lax.broadcasted_iota(jnp.int32, cqb.shape, 1)
        cq = [jnp.sum(jnp.where(lane == f_lane + g * hg + h, cqb, 0.0), axis=1, keepdims=True)
              for h in heads]
        ck = [ck_ref[0, pl.ds(f_lane + g * hg + h, 1), :] * LOG2E for h in heads]
        qb = [(q_ref[0, :, sl] * q_scale).astype(BF16) for sl in sls]
        kb = [k_ref[0, :, sl].astype(BF16) for sl in sls]
        x = [_dot_nt(a, b) - c for a, b, c in zip(qb, kb, ck)]
        if masked:
            qpos = qi * tq + q_off + lax.broadcasted_iota(jnp.int32, (tq, tk), 0)
            kpos = kj * tk + lax.broadcasted_iota(jnp.int32, (tq, tk), 1)
            keep = kpos <= qpos
            x = [jnp.where(keep, a, NEG_INF) for a in x]
        m_old = [m_scr[h] for h in heads]
        m_new = [jnp.maximum(mo, c + jnp.max(a, axis=1, keepdims=True)) for mo, c, a in zip(m_old, cq, x)]
        p = [jnp.exp2(a + (c - mn)) for a, c, mn in zip(x, cq, m_new)]
        alpha = [jnp.exp2(mo - mn) for mo, mn in zip(m_old, m_new)]
        pv = [_dot(a.astype(BF16), v_ref[0, :, sl].astype(BF16)) for a, sl in zip(p, sls)]
        for h in heads:
            l_scr[h] = alpha[h] * l_scr[h] + jnp.sum(p[h], axis=1, keepdims=True)
            acc_scr[h] = alpha[h] * acc_scr[h] + pv[h]
            m_scr[h] = m_new[h]

    active = kj * tk <= qi * tq + (tq - 1) + q_off
    crosses = kj * tk + (tk - 1) > qi * tq + q_off

    @pl.when(active & crosses)
    def _():
        update(True)

    @pl.when(active & jnp.logical_not(crosses))
    def _():
        update(False)

    @pl.when(kj == pl.num_programs(3) - 1)
    def _():
        for h in heads:
            o_ref[0, :, sls[h]] = (acc_scr[h] / l_scr[h]).astype(o_ref.dtype)


def _fox(q_arr, q_blk, k_arr, k_blk, v_arr, v_blk, cq_col, ck_row, *, bh, hd, f_lane):
    b, t_q, _ = q_arr.shape
    t_k = k_arr.shape[1]
    q_off = t_k - t_q
    tq = _tile(t_q, 512, 8)
    tk = _tile(t_k, 512)
    hg = 4 if bh % 4 == 0 else bh
    assert q_blk % hg == 0 and k_blk % hg == 0 and v_blk % hg == 0
    wg = hg * hd
    last = lambda qi: (qi * tq + (tq - 1) + q_off) // tk
    return pl.pallas_call(
        functools.partial(_fox_kernel, q_off, f_lane, hg, hd),
        grid=(b, bh // hg, t_q // tq, t_k // tk),
        in_specs=[
            pl.BlockSpec((1, tq, wg), lambda i, g, qi, kj: (i, qi, q_blk // hg + g)),
            pl.BlockSpec((1, tk, wg), lambda i, g, qi, kj: (i, jnp.minimum(kj, last(qi)), k_blk // hg + g)),
            pl.BlockSpec((1, tk, wg), lambda i, g, qi, kj: (i, jnp.minimum(kj, last(qi)), v_blk // hg + g)),
            pl.BlockSpec((1, tq, LANE), lambda i, g, qi, kj: (i, qi, 0)),
            pl.BlockSpec((1, LANE, tk), lambda i, g, qi, kj: (i, 0, jnp.minimum(kj, last(qi)))),
        ],
        out_specs=pl.BlockSpec((1, tq, wg), lambda i, g, qi, kj: (i, qi, g)),
        out_shape=jax.ShapeDtypeStruct((b, t_q, bh * hd), BF16),
        scratch_shapes=[
            pltpu.VMEM((hg, tq, 1), F32),
            pltpu.VMEM((hg, tq, 1), F32),
            pltpu.VMEM((hg, tq, hd), F32),
        ],
        compiler_params=_params(("parallel", "parallel", "parallel", "arbitrary")),
        name="fox",
    )(q_arr, k_arr, v_arr, cq_col, ck_row)


def _layer_norm_rows(x, g, b):
    mu = jnp.mean(x, axis=1, keepdims=True)
    xc = x - mu
    var = jnp.mean(xc * xc, axis=1, keepdims=True)
    return xc * lax.rsqrt(var + LN_EPS) * g + b


def _layer_norm_ref(ref, g_ref, b_ref, rows):
    def body(r, carry):
        sl = pl.ds(pl.multiple_of(r * rows, rows), rows)
        ref[sl, :] = _layer_norm_rows(ref[sl, :], g_ref[...], b_ref[...])
        return carry
    lax.fori_loop(0, ref.shape[0] // rows, body, 0)


def _outproj_kernel(alpha, tn, oa_ref, ob_ref, wa_ref, wb_ref, x_ref, g_ref, b_ref, hid_ref, hidt_ref):
    j = pl.program_id(1)
    col = pl.multiple_of(j * tn, LANE)
    hid_ref[:, pl.ds(col, tn)] = (alpha * x_ref[...] + _dot(oa_ref[...], wa_ref[...])
                                  + _dot(ob_ref[...], wb_ref[...]))

    @pl.when(j == pl.num_programs(1) - 1)
    def _():
        tm, d = hid_ref.shape
        _layer_norm_ref(hid_ref, g_ref, b_ref, min(tm, 32))
        rb = min(tm, LANE)
        step = _tile(d, 512)
        for r in range(tm // rb):
            for c in range(d // step):
                hidt_ref[c * step:(c + 1) * step, r * rb:(r + 1) * rb] = (
                    hid_ref[r * rb:(r + 1) * rb, c * step:(c + 1) * step].T.astype(BF16))


def _outproj(o_a, o_b, w_a, w_b, x2d, g, b, alpha):
    m, d = x2d.shape
    tm = _tile(m, 512)
    tn = _tile(d, 512)
    return pl.pallas_call(
        functools.partial(_outproj_kernel, alpha, tn),
        grid=(m // tm, d // tn),
        in_specs=[
            pl.BlockSpec((tm, o_a.shape[1]), lambda i, j: (i, 0)),
            pl.BlockSpec((tm, o_b.shape[1]), lambda i, j: (i, 0)),
            pl.BlockSpec((w_a.shape[0], tn), lambda i, j: (0, j)),
            pl.BlockSpec((w_b.shape[0], tn), lambda i, j: (0, j)),
            pl.BlockSpec((tm, tn), lambda i, j: (i, j)),
            pl.BlockSpec((1, d), lambda i, j: (0, 0)),
            pl.BlockSpec((1, d), lambda i, j: (0, 0)),
        ],
        out_specs=[
            pl.BlockSpec((tm, d), lambda i, j: (i, 0)),
            pl.BlockSpec((d, tm), lambda i, j: (0, i)),
        ],
        out_shape=[
            jax.ShapeDtypeStruct((m, d), F32),
            jax.ShapeDtypeStruct((d, m), BF16),
        ],
        compiler_params=_params(("parallel", "arbitrary")),
        name="outproj",
    )(o_a, o_b, w_a, w_b, x2d, g, b)


def _top_values(x, n):
    vals = []
    for _ in range(n):
        m = jnp.max(x, axis=0, keepdims=True)
        vals.append(m)
        x = jnp.where(x == m, NEG_INF, x)
    return vals


def _route_kernel(topk, ht_ref, wq_ref, key_ref, a1_ref, a2_ref, thr_ref, cand_scr):
    dk = key_ref.shape[3]
    n = topk + 1
    qt = _dot(wq_ref[...], ht_ref[...])
    s1 = _dot(key_ref[0, 0], qt[:dk].astype(BF16))
    s2 = _dot(key_ref[0, 1], qt[dk:].astype(BF16))
    top1 = _top_values(s1, n)
    top2 = _top_values(s2, n)
    cand_scr[...] = jnp.full(cand_scr.shape, NEG_INF, F32)
    pairs = [(a, b) for a in range(n) for b in range(n) if (a + 1) * (b + 1) <= n]
    for r, (a, b) in enumerate(pairs):
        cand_scr[r:r + 1, :] = top1[a] + top2[b]
    best = _top_values(cand_scr[...], n)
    z = jnp.exp(best[0] - best[0])
    for t in best[1:topk]:
        z = z + jnp.exp(t - best[0])
    a1_ref[0] = jnp.exp(s1 - top1[0]) / z
    a2_ref[0] = jnp.exp(s2 - top2[0])
    thr = jnp.exp(0.5 * (best[topk - 1] + best[topk]) - best[0]) / z
    thr_ref[0] = jnp.broadcast_to(thr, thr_ref.shape[1:])


def _num_candidates(n):
    return -(-sum(n // (a + 1) for a in range(n)) // 8) * 8


def _route(hid_t, wq_t, keys, topk):
    d, m = hid_t.shape
    ph, _, nk, dk = keys.shape
    tm = _tile(m, 512)
    out = jax.ShapeDtypeStruct((ph, nk, m), F32)
    blk = pl.BlockSpec((1, nk, tm), lambda i, h: (h, 0, i))
    return pl.pallas_call(
        functools.partial(_route_kernel, topk),
        grid=(m // tm, ph),
        in_specs=[
            pl.BlockSpec((d, tm), lambda i, h: (0, i)),
            pl.BlockSpec((2 * dk, d), lambda i, h: (h, 0)),
            pl.BlockSpec((1, 2, nk, dk), lambda i, h: (h, 0, 0, 0)),
        ],
        out_specs=[blk, blk, pl.BlockSpec((1, 8, tm), lambda i, h: (h, 0, i))],
        out_shape=[out, out, jax.ShapeDtypeStruct((ph, 8, m), F32)],
        scratch_shapes=[pltpu.VMEM((_num_candidates(topk + 1), tm), F32)],
        compiler_params=_params(("parallel", "arbitrary")),
        name="peer_route",
    )(hid_t, wq_t, keys)


def _peer_kernel(ht_ref, u_ref, vt_ref, a1_ref, a2_ref, thr_ref, o_ref, act_scr):
    e = pl.program_id(1)
    ph, nk, _ = a2_ref.shape
    te = u_ref.shape[0]
    n_sub = te // nk

    @pl.when(e == 0)
    def _():
        o_ref[...] = jnp.zeros(o_ref.shape, F32)

    pre = _dot(u_ref[...], ht_ref[...])
    for sub in range(n_sub):
        row = e * n_sub + sub
        gate = None
        for h in range(ph):
            g = a2_ref[h] * a1_ref[h, pl.ds(row, 1), :]
            term = jnp.where(g >= thr_ref[h, 0:1, :], g, 0.0)
            gate = term if gate is None else gate + term
        x = pre[sub * nk:(sub + 1) * nk]
        act = gate * (0.5 * x * (1.0 + lax.erf(x * (2.0 ** -0.5))))
        act_scr[sub * nk:(sub + 1) * nk, :] = act.astype(BF16)
    d = o_ref.shape[0]
    step = _tile(d, 1024)

    def body(c, carry):
        sl = pl.ds(pl.multiple_of(c * step, step), step)
        o_ref[sl, :] += _dot(vt_ref[sl, :], act_scr[...])
        return carry
    lax.fori_loop(0, d // step, body, 0)


def _peer(hid_t, u_b, v_t, a1, a2, thr):
    d, m = hid_t.shape
    ne = u_b.shape[0]
    ph, nk, _ = a1.shape
    tm = _tile(m, 512)
    te = _tile(ne, 512)
    rt = pl.BlockSpec((ph, nk, tm), lambda i, e: (0, 0, i))
    return pl.pallas_call(
        _peer_kernel,
        grid=(m // tm, ne // te),
        in_specs=[
            pl.BlockSpec((d, tm), lambda i, e: (0, i)),
            pl.BlockSpec((te, d), lambda i, e: (e, 0)),
            pl.BlockSpec((d, te), lambda i, e: (0, e)),
            rt, rt,
            pl.BlockSpec((ph, 8, tm), lambda i, e: (0, 0, i)),
        ],
        out_specs=pl.BlockSpec((d, tm), lambda i, e: (0, i)),
        out_shape=jax.ShapeDtypeStruct((d, m), F32),
        scratch_shapes=[pltpu.VMEM((te, tm), BF16)],
        compiler_params=_params(("parallel", "arbitrary")),
        name="peer_dense",
    )(hid_t, u_b, v_t, a1, a2, thr)


def _final_kernel(alpha, hid_ref, pt_ref, g_ref, b_ref, y_ref):
    d = hid_ref.shape[1]
    step = _tile(d, 512)
    for c in range(d // step):
        sl = slice(c * step, (c + 1) * step)
        y_ref[:, sl] = alpha * hid_ref[:, sl] + pt_ref[sl, :].T
    _layer_norm_ref(y_ref, g_ref, b_ref, min(y_ref.shape[0], 32))


def _final(hid, peer_t, g, b, alpha):
    m, d = hid.shape
    tm = _tile(m, 256)
    return pl.pallas_call(
        functools.partial(_final_kernel, alpha),
        grid=(m // tm,),
        in_specs=[
            pl.BlockSpec((tm, d), lambda i: (i, 0)),
            pl.BlockSpec((d, tm), lambda i: (0, i)),
            pl.BlockSpec((1, d), lambda i: (0, 0)),
            pl.BlockSpec((1, d), lambda i: (0, 0)),
        ],
        out_specs=pl.BlockSpec((tm, d), lambda i: (i, 0)),
        out_shape=jax.ShapeDtypeStruct((m, d), F32),
        compiler_params=_params(("parallel",)),
        name="final_ln",
    )(hid, peer_t, g, b)


def _pad_lanes(a):
    return jnp.pad(a, [(0, 0)] * (a.ndim - 1) + [(0, LANE - a.shape[-1])])


def _trunk(x, conv_hist, s0, fox_cache, wts, depth):
    (w_main, w_small, conv_w, prow, pcol, norm_w, w_a, w_b, ln1_g, ln1_b, wq_t, keys, u_b, v_t,
     ln2_g, ln2_b, ah, bh, hd, topk) = wts
    b, t_len, d = x.shape
    m = b * t_len
    aw, bw = ah * hd, bh * hd
    alpha = (2 * depth) ** 0.25
    f_lane = 2 * ah
    x2d = x.reshape(m, d)

    proj, sm, smt = _inproj(x2d, w_main, w_small)
    proj3 = proj.reshape(b, t_len, -1)
    n_a = 4 * aw
    k_new = proj3[:, :, n_a + bw:n_a + 2 * bw]
    v_new = proj3[:, :, n_a + 2 * bw:n_a + 3 * bw]

    c_len = min(64, t_len)
    zero_c = jnp.zeros((b, 1, LANE), F32)
    zero_r = jnp.zeros((b, LANE, 1), F32)
    if fox_cache is None:
        carry_c, carry_r = zero_c, zero_r
    else:
        clf = fox_cache[2].astype(F32)
        p_len = clf.shape[1]
        clf_col = jnp.pad(clf, ((0, 0), (0, 0), (f_lane, LANE - f_lane - bh)))
        clf_row = jnp.swapaxes(clf_col, 1, 2)
        cc_col, _, cc_row = _gates(clf_col, clf_row, zero_c, zero_r, prow, pcol,
                                   apply=False, c_len=min(64, p_len), ah=ah, bh=bh)
        carry_c = cc_col[:, p_len - 1:, :]
        carry_r = cc_row[:, :, p_len - 1:]
    smt3 = jnp.swapaxes(smt.reshape(LANE, b, t_len), 0, 1)
    col, rowc, rowf = _gates(sm.reshape(b, t_len, LANE), smt3, carry_c, carry_r, prow, pcol,
                             apply=True, c_len=c_len, ah=ah, bh=bh)
    logf = col[:, :, f_lane + bh:f_lane + 2 * bh]

    o_a, s_new = _gdn(proj3, conv_hist, conv_w, col, rowc, s0, norm_w, c_len=c_len, ah=ah, hd=hd)
    conv_new = jnp.concatenate([conv_hist.astype(F32), proj3[:, :, :3 * aw]], axis=1)[:, -3:]

    qb0 = n_a // hd
    if fox_cache is None:
        o_b = _fox(proj3, qb0, proj3, qb0 + bh, proj3, qb0 + 2 * bh, col, rowf, bh=bh, hd=hd, f_lane=f_lane)
    else:
        k_all = jnp.concatenate([fox_cache[0].reshape(b, -1, bw).astype(F32), k_new], axis=1)
        v_all = jnp.concatenate([fox_cache[1].reshape(b, -1, bw).astype(F32), v_new], axis=1)
        ck_row = jnp.concatenate([cc_row, rowf], axis=2)
        o_b = _fox(proj3, qb0, k_all, 0, v_all, 0, col, ck_row, bh=bh, hd=hd, f_lane=f_lane)

    hid, hid_t = _outproj(o_a.reshape(m, aw), o_b.reshape(m, bw), w_a, w_b, x2d, ln1_g, ln1_b, alpha)
    a1, a2, thr = _route(hid_t, wq_t, keys, topk)
    peer_t = _peer(hid_t, u_b, v_t, a1, a2, thr)
    y = _final(hid, peer_t, ln2_g, ln2_b, alpha).reshape(b, t_len, d)
    return y, (k_new.reshape(b, t_len, bh, hd), v_new.reshape(b, t_len, bh, hd), logf, s_new, conv_new)


def kernel(x_prompt, x_sample, cache_fox_k, cache_fox_v, cache_fox_logf, state_gdn, state_gdn_conv,
           w_in, gdn_conv_w, gdn_a_log, gdn_dt_bias, gdn_norm_w, fox_f_bias, w_out, ln1_g, ln1_b,
           peer_w_q, peer_sub_keys, peer_u, peer_v, ln2_g, ln2_b):
    depth = w_in.shape[0]
    ah = gdn_a_log.shape[1]
    bh = fox_f_bias.shape[1]
    hd = gdn_norm_w.shape[1]
    aw, bw = ah * hd, bh * hd
    topk = 16
    assert 2 * ah + 2 * bh <= LANE
    n_p = x_prompt.shape[0]
    yp, ys = x_prompt, x_sample
    outs_p, outs_s = [], []
    for l in range(depth):
        o_a_a = 4 * aw
        o_b_qkv = o_a_a + 2 * ah
        o_b_f = o_b_qkv + 3 * bw
        wl = w_in[l]
        w_main = jnp.concatenate([wl[:, :o_a_a], wl[:, o_b_qkv:o_b_f]], axis=1).astype(BF16)
        w_f = wl[:, o_b_f:o_b_f + bh]
        w_small = _pad_lanes(jnp.concatenate([wl[:, o_a_a:o_b_qkv], w_f, w_f], axis=1)).astype(BF16)
        zeros_a = jnp.zeros((ah,), F32)
        prow = jnp.stack([
            _pad_lanes(gdn_a_log[l].astype(F32)),
            _pad_lanes(jnp.concatenate([gdn_dt_bias[l].astype(F32), zeros_a,
                                        fox_f_bias[l].astype(F32), fox_f_bias[l].astype(F32)])),
        ])
        wts = (w_main, w_small, gdn_conv_w[l].astype(F32), prow, prow.T,
               gdn_norm_w[l].reshape(1, hd).astype(F32),
               w_out[l][:aw].astype(BF16), w_out[l][aw:].astype(BF16),
               ln1_g[l].reshape(1, -1), ln1_b[l].reshape(1, -1),
               peer_w_q[l].T.astype(BF16), peer_sub_keys[l].astype(BF16),
               peer_u[l].astype(BF16), peer_v[l].T.astype(BF16),
               ln2_g[l].reshape(1, -1), ln2_b[l].reshape(1, -1), ah, bh, hd, topk)
        conv0 = jnp.zeros((n_p, 3, 3 * aw), yp.dtype)
        s0 = jnp.zeros((n_p, ah, hd, hd), yp.dtype)
        yp, st_p = _trunk(yp, conv0, s0, None, wts, depth)
        ys, st_s = _trunk(ys, state_gdn_conv[l], state_gdn[l],
                          (cache_fox_k[l], cache_fox_v[l], cache_fox_logf[l]), wts, depth)
        outs_p.append(st_p)
        outs_s.append(st_s)
    stack = lambda outs, n: jnp.stack([o[n] for o in outs], axis=0)
    return ((yp, ys) + tuple(stack(outs_p, n) for n in range(5))
            + tuple(stack(outs_s, n) for n in range(5)))
```

```python
import functools

import jax
import jax.numpy as jnp
from jax import lax
from jax.experimental import pallas as pl
from jax.experimental.pallas import tpu as pltpu

F32 = jnp.float32
BF16 = jnp.bfloat16
LANE = 128
LN_EPS = 1e-5
RMS_EPS = 1e-6
L2_EPS = 1e-6
CONV_PAD = 8
VMEM_LIMIT = 58 * 1024 * 1024
HIGHEST = lax.Precision.HIGHEST
NEG_INF = float("-inf")
LOG2E = 1.4426950408889634


def _params(sem):
    return pltpu.CompilerParams(dimension_semantics=sem, vmem_limit_bytes=VMEM_LIMIT)


def _tile(n, pref, align=LANE):
    if n <= pref:
        return n
    t = (pref // align) * align
    while t >= align:
        if n % t == 0:
            return t
        t -= align
    return n


def _dot(a, b):
    return jnp.dot(a, b, preferred_element_type=F32)


def _dot_nt(a, b):
    return lax.dot_general(a, b, (((1,), (1,)), ((), ())), preferred_element_type=F32)


def _dot_tn(a, b):
    return lax.dot_general(a, b, (((0,), (0,)), ((), ())), preferred_element_type=F32)


def _sigmoid(x):
    return 1.0 / (1.0 + jnp.exp(-x))


def _softplus(x):
    return jnp.maximum(x, 0.0) + jnp.log1p(jnp.exp(-jnp.abs(x)))


def _silu(x):
    return x * _sigmoid(x)


def _inproj_kernel(j_q, j_k, j_v, emit_t, x_ref, w_ref, ws_ref, o_ref, k_ref, v_ref, s_ref, st_ref, *rest):
    xb_ref = rest[-1]
    j = pl.program_id(1)

    @pl.when(j == 0)
    def _():
        xb = x_ref[...].astype(BF16)
        xb_ref[...] = xb
        sm = _dot(xb, ws_ref[...])
        s_ref[...] = sm
        st_ref[...] = sm.T

    acc = _dot(xb_ref[...], w_ref[...])

    @pl.when(j < j_k)
    def _():
        o_ref[...] = acc

    @pl.when((j >= j_k) & (j < j_v))
    def _():
        k_ref[...] = acc

    @pl.when(j >= j_v)
    def _():
        v_ref[...] = acc

    if emit_t:
        @pl.when(((j >= j_q) & (j < j_k)) | (j >= j_v))
        def _():
            rest[0][...] = acc.T.astype(BF16)


def _inproj(x2d, w_main, w_small, n_a, bw, emit_t):
    m, d = x2d.shape
    n = w_main.shape[1]
    assert n == n_a + 3 * bw
    tm = _tile(m, 512)
    tn = _tile(bw, 1024)
    assert n_a % tn == 0 and bw % tn == 0
    nb = bw // tn
    j_q = n_a // tn
    j_k = j_q + nb
    j_v = j_k + nb
    out_specs = [
        pl.BlockSpec((tm, tn), lambda i, j: (i, jnp.minimum(j, j_k - 1))),
        pl.BlockSpec((tm, tn), lambda i, j: (i, jnp.clip(j - j_k, 0, nb - 1))),
        pl.BlockSpec((tm, tn), lambda i, j: (i, jnp.maximum(j - j_v, 0))),
        pl.BlockSpec((tm, LANE), lambda i, j: (i, 0)),
        pl.BlockSpec((LANE, tm), lambda i, j: (0, i)),
    ]
    out_shape = [
        jax.ShapeDtypeStruct((m, n_a + bw), F32),
        jax.ShapeDtypeStruct((m, bw), F32),
        jax.ShapeDtypeStruct((m, bw), F32),
        jax.ShapeDtypeStruct((m, LANE), F32),
        jax.ShapeDtypeStruct((LANE, m), F32),
    ]
    if emit_t:
        row_blk = lambda j: jnp.where(j < j_k, jnp.maximum(j - j_q, 0),
                                      jnp.where(j < j_v, nb - 1, j - j_v + nb))
        out_specs.append(pl.BlockSpec((tn, tm), lambda i, j: (row_blk(j), i)))
        out_shape.append(jax.ShapeDtypeStruct((2 * bw, m), BF16))
    return pl.pallas_call(
        functools.partial(_inproj_kernel, j_q, j_k, j_v, emit_t),
        grid=(m // tm, n // tn),
        in_specs=[
            pl.BlockSpec((tm, d), lambda i, j: (i, 0)),
            pl.BlockSpec((d, tn), lambda i, j: (0, j)),
            pl.BlockSpec((d, LANE), lambda i, j: (0, 0)),
        ],
        out_specs=out_specs,
        out_shape=out_shape,
        scratch_shapes=[pltpu.VMEM((tm, d), BF16)],
        compiler_params=_params(("parallel", "arbitrary")),
        name="inproj",
    )(x2d, w_main, w_small)


def _gate_values(z, a_log, bias, idx, ah, bh):
    zz = z + bias
    g = -jnp.exp(a_log) * _softplus(zz)
    beta = _sigmoid(zz)
    logf = -_softplus(-zz)
    return jnp.where(idx < ah, g, jnp.where(idx < 2 * ah, beta, jnp.where(idx < 2 * ah + 2 * bh, logf, 0.0)))


def _gate_merge(idx, cs, y, carry, ah, bh):
    return jnp.where(idx < ah, cs,
                     jnp.where(idx < 2 * ah, y,
                               jnp.where(idx < 2 * ah + bh, cs + carry,
                                         jnp.where(idx < 2 * ah + 2 * bh, y, 0.0))))


def _gates_kernel(apply, c_len, ah, bh, sm_ref, smt_ref, cc_ref, cr_ref, prow_ref, pcol_ref,
                  col_ref, rowc_ref, rowf_ref):
    t_len = sm_ref.shape[1]
    nc = t_len // c_len
    ii = lax.broadcasted_iota(jnp.int32, (c_len, c_len), 0)
    jj = lax.broadcasted_iota(jnp.int32, (c_len, c_len), 1)
    tril = (ii >= jj).astype(F32)
    triu = (ii <= jj).astype(F32)
    lane = lax.broadcasted_iota(jnp.int32, (c_len, LANE), 1)
    subl = lax.broadcasted_iota(jnp.int32, (LANE, c_len), 0)
    carry_c = cc_ref[0]
    carry_r = cr_ref[0]
    for c in range(nc):
        z = sm_ref[0, c * c_len:(c + 1) * c_len, :]
        y = _gate_values(z, prow_ref[0:1, :], prow_ref[1:2, :], lane, ah, bh) if apply else z
        cs = jnp.dot(tril, y, precision=HIGHEST, preferred_element_type=F32)
        col_ref[0, c * c_len:(c + 1) * c_len, :] = _gate_merge(lane, cs, y, carry_c, ah, bh)
        carry_c = carry_c + cs[c_len - 1:c_len, :]

        zt = smt_ref[0, :, c * c_len:(c + 1) * c_len]
        yt = _gate_values(zt, pcol_ref[:, 0:1], pcol_ref[:, 1:2], subl, ah, bh) if apply else zt
        cst = jnp.dot(yt, triu, precision=HIGHEST, preferred_element_type=F32)
        out_t = _gate_merge(subl, cst, yt, carry_r, ah, bh)
        rowc_ref[0, c] = out_t
        rowf_ref[0, :, c * c_len:(c + 1) * c_len] = out_t
        carry_r = carry_r + cst[:, c_len - 1:c_len]


def _gates(sm3, smt, carry_col, carry_row, prow, pcol, *, apply, c_len, ah, bh):
    b, t_len, _ = sm3.shape
    nc = t_len // c_len
    return pl.pallas_call(
        functools.partial(_gates_kernel, apply, c_len, ah, bh),
        grid=(b,),
        in_specs=[
            pl.BlockSpec((1, t_len, LANE), lambda i: (i, 0, 0)),
            pl.BlockSpec((1, LANE, t_len), lambda i: (i, 0, 0)),
            pl.BlockSpec((1, 1, LANE), lambda i: (i, 0, 0)),
            pl.BlockSpec((1, LANE, 1), lambda i: (i, 0, 0)),
            pl.BlockSpec((2, LANE), lambda i: (0, 0)),
            pl.BlockSpec((LANE, 2), lambda i: (0, 0)),
        ],
        out_specs=[
            pl.BlockSpec((1, t_len, LANE), lambda i: (i, 0, 0)),
            pl.BlockSpec((1, nc, LANE, c_len), lambda i: (i, 0, 0, 0)),
            pl.BlockSpec((1, LANE, t_len), lambda i: (i, 0, 0)),
        ],
        out_shape=[
            jax.ShapeDtypeStruct((b, t_len, LANE), F32),
            jax.ShapeDtypeStruct((b, nc, LANE, c_len), F32),
            jax.ShapeDtypeStruct((b, LANE, t_len), F32),
        ],
        compiler_params=_params(("parallel",)),
        name="gates",
    )(sm3, smt, carry_col, carry_row, prow, pcol)


def _unit_lower_inverses(lows, eye, ii, jj):
    c_len = lows[0].shape[0]
    same0 = (ii >> 1) == (jj >> 1)
    ts = [eye - jnp.where(same0, low, 0.0) for low in lows]
    k = 1
    while (2 << k) <= c_len:
        sel = ((ii >> (k + 1)) == (jj >> (k + 1))) & (((ii >> k) & 1) == 1) & (((jj >> k) & 1) == 0)
        tbs = [t.astype(BF16) for t in ts]
        mid = [_dot(tb, jnp.where(sel, low, 0.0).astype(BF16)).astype(BF16) for tb, low in zip(tbs, lows)]
        ts = [t - _dot(m, tb) for t, m, tb in zip(ts, mid, tbs)]
        k += 1
    resid = [(eye - jnp.dot(eye + low, t, precision=HIGHEST, preferred_element_type=F32)).astype(BF16)
             for low, t in zip(lows, ts)]
    return [t + _dot(t.astype(BF16), r) for t, r in zip(ts, resid)]


def _gdn_kernel(c_len, hb, ah, q_ref, k_ref, v_ref, z_ref, hq_ref, hk_ref, hv_ref,
                cq_ref, ck_ref, cv_ref, col_ref, rowc_ref, s0_ref, nw_ref,
                o_ref, sn_ref, s_scr, buf_scr):
    c_idx = pl.program_id(2)
    hd = nw_ref.shape[1]
    lo = CONV_PAD - 3

    @pl.when(c_idx == 0)
    def _():
        s_scr[...] = s0_ref[0]
        buf_scr[0, lo:CONV_PAD, :] = hq_ref[0]
        buf_scr[1, lo:CONV_PAD, :] = hk_ref[0]
        buf_scr[2, lo:CONV_PAD, :] = hv_ref[0]

    conv = []
    for n, (x_ref, w_ref) in enumerate(((q_ref, cq_ref), (k_ref, ck_ref), (v_ref, cv_ref))):
        buf_scr[n, CONV_PAD:CONV_PAD + c_len, :] = x_ref[0]
        acc = w_ref[0:1, :] * buf_scr[n, lo:lo + c_len, :]
        for w in range(1, 4):
            acc = acc + w_ref[w:w + 1, :] * buf_scr[n, lo + w:lo + w + c_len, :]
        buf_scr[n, lo:CONV_PAD, :] = buf_scr[n, lo + c_len:CONV_PAD + c_len, :]
        conv.append(_silu(acc))
    qc, kc, vc = conv

    colblk = col_ref[0]
    ii = lax.broadcasted_iota(jnp.int32, (c_len, c_len), 0)
    jj = lax.broadcasted_iota(jnp.int32, (c_len, c_len), 1)
    eye = (ii == jj).astype(F32)
    nw = nw_ref[...]
    heads = range(hb)
    sls = [slice(h * hd, (h + 1) * hd) for h in heads]

    cum_c = [colblk[:, h:h + 1] for h in heads]
    beta_c = [colblk[:, ah + h:ah + h + 1] for h in heads]
    cum_r = [rowc_ref[0, 0, h:h + 1, :] for h in heads]
    cum_last = [r[:, c_len - 1:c_len] for r in cum_r]
    e_cum = [jnp.exp(c) for c in cum_c]

    q = [qc[:, sl] for sl in sls]
    k = [kc[:, sl] for sl in sls]
    q = [x * (lax.rsqrt(jnp.sum(x * x, axis=1, keepdims=True) + L2_EPS) * (hd ** -0.5)) for x in q]
    k = [x * lax.rsqrt(jnp.sum(x * x, axis=1, keepdims=True) + L2_EPS) for x in k]
    qb = [x.astype(BF16) for x in q]
    kb = [x.astype(BF16) for x in k]

    decay = [jnp.exp(jnp.where(ii >= jj, c - r, NEG_INF)) for c, r in zip(cum_c, cum_r)]
    kk = [_dot_nt(x, x) for x in kb]
    qk = [_dot_nt(x, y) for x, y in zip(qb, kb)]
    lows = [jnp.where(ii > jj, b * m * d, 0.0) for b, m, d in zip(beta_c, kk, decay)]
    t_inv = _unit_lower_inverses(lows, eye, ii, jj)

    rhs = [jnp.concatenate([b * vc[:, sl], (b * e) * x], axis=1).astype(BF16)
           for b, e, x, sl in zip(beta_c, e_cum, k, sls)]
    w_all = [_dot(t.astype(BF16), r) for t, r in zip(t_inv, rhs)]

    s_old = [s_scr[h] for h in heads]
    sb = [s.astype(BF16) for s in s_old]
    u = [w[:, :hd] - _dot(w[:, hd:].astype(BF16), s) for w, s in zip(w_all, sb)]
    ub = [x.astype(BF16) for x in u]
    o = [_dot((x * e).astype(BF16), s) + _dot((m * d).astype(BF16), y)
         for x, e, s, m, d, y in zip(q, e_cum, sb, qk, decay, ub)]
    for h in heads:
        k_dec = k[h] * jnp.exp(cum_last[h] - cum_c[h])
        s_scr[h] = s_old[h] * jnp.exp(cum_last[h]) + _dot_tn(k_dec.astype(BF16), ub[h])
    for h in heads:
        x = o[h] * lax.rsqrt(jnp.mean(o[h] * o[h], axis=1, keepdims=True) + RMS_EPS) * nw
        o_ref[0, :, sls[h]] = (x * _silu(z_ref[0, :, sls[h]])).astype(o_ref.dtype)

    @pl.when(c_idx == pl.num_programs(2) - 1)
    def _():
        sn_ref[0] = s_scr[...]


def _gdn(proj3, conv_hist, conv_w, col, rowc, s0, norm_w, *, c_len, ah, hd):
    b, t_len, _ = proj3.shape
    nc = t_len // c_len
    aw = ah * hd
    hb = ah
    ng = ah // hb
    wb = hb * hd
    kq, kk_, kv, kz = 0, ng, 2 * ng, 3 * ng
    tok = lambda off: pl.BlockSpec((1, c_len, wb), lambda i, g, c: (i, c, off + g))
    hist = lambda off: pl.BlockSpec((1, 3, wb), lambda i, g, c: (i, 0, off + g))
    cw = lambda off: pl.BlockSpec((4, wb), lambda i, g, c: (0, off + g))
    return pl.pallas_call(
        functools.partial(_gdn_kernel, c_len, hb, ah),
        grid=(b, ng, nc),
        in_specs=[
            tok(kq), tok(kk_), tok(kv), tok(kz),
            hist(kq), hist(kk_), hist(kv),
            cw(kq), cw(kk_), cw(kv),
            pl.BlockSpec((1, c_len, LANE), lambda i, g, c: (i, c, 0)),
            pl.BlockSpec((1, 1, LANE, c_len), lambda i, g, c: (i, c, 0, 0)),
            pl.BlockSpec((1, hb, hd, hd), lambda i, g, c: (i, g, 0, 0)),
            pl.BlockSpec((1, hd), lambda i, g, c: (0, 0)),
        ],
        out_specs=[
            pl.BlockSpec((1, c_len, wb), lambda i, g, c: (i, c, g)),
            pl.BlockSpec((1, hb, hd, hd), lambda i, g, c: (i, g, 0, 0)),
        ],
        out_shape=[
            jax.ShapeDtypeStruct((b, t_len, aw), BF16),
            jax.ShapeDtypeStruct((b, ah, hd, hd), F32),
        ],
        scratch_shapes=[
            pltpu.VMEM((hb, hd, hd), F32),
            pltpu.VMEM((3, CONV_PAD + c_len, wb), F32),
        ],
        compiler_params=_params(("parallel", "parallel", "arbitrary")),
        name="gdn",
    )(proj3, proj3, proj3, proj3, conv_hist, conv_hist, conv_hist, conv_w, conv_w, conv_w,
      col, rowc, s0, norm_w)


def _fox_kernel(q_off, f_lane, hg, hd, q_ref, k_ref, v_ref, cq_ref, ck_ref, o_ref, m_scr, l_scr, acc_scr):
    g = pl.program_id(1)
    qi = pl.program_id(2)
    kj = pl.program_id(3)
    tq = q_ref.shape[1]
    tk = k_ref.shape[1]
    heads = range(hg)
    sls = [slice(h * hd, (h + 1) * hd) for h in heads]
    q_scale = (hd ** -0.5) * LOG2E

    @pl.when(kj == 0)
    def _():
        m_scr[...] = jnp.full(m_scr.shape, NEG_INF, F32)
        l_scr[...] = jnp.zeros(l_scr.shape, F32)
        acc_scr[...] = jnp.zeros(acc_scr.shape, F32)

    def update(masked):
        cqb = cq_ref[0] * LOG2E
        lane = lax.broadcasted_iota(jnp.int32, cqb.shape, 1)
        cq = [jnp.sum(jnp.where(lane == f_lane + g * hg + h, cqb, 0.0), axis=1, keepdims=True)
              for h in heads]
        ck = [ck_ref[0, pl.ds(f_lane + g * hg + h, 1), :] * LOG2E for h in heads]
        qb = [(q_ref[0, :, sl] * q_scale).astype(BF16) for sl in sls]
        kb = [k_ref[0, :, sl].astype(BF16) for sl in sls]
        x = [_dot_nt(a, b) - c for a, b, c in zip(qb, kb, ck)]
        if masked:
            qpos = qi * tq + q_off + lax.broadcasted_iota(jnp.int32, (tq, tk), 0)
            kpos = kj * tk + lax.broadcasted_iota(jnp.int32, (tq, tk), 1)
            keep = kpos <= qpos
            x = [jnp.where(keep, a, NEG_INF) for a in x]
        m_old = [m_scr[h] for h in heads]
        m_new = [jnp.maximum(mo, c + jnp.max(a, axis=1, keepdims=True)) for mo, c, a in zip(m_old, cq, x)]
        p = [jnp.exp2(a + (c - mn)) for a, c, mn in zip(x, cq, m_new)]
        alpha = [jnp.exp2(mo - mn) for mo, mn in zip(m_old, m_new)]
        pv = [_dot(a.astype(BF16), v_ref[0, :, sl].astype(BF16)) for a, sl in zip(p, sls)]
        for h in heads:
            l_scr[h] = alpha[h] * l_scr[h] + jnp.sum(p[h], axis=1, keepdims=True)
            acc_scr[h] = alpha[h] * acc_scr[h] + pv[h]
            m_scr[h] = m_new[h]

    active = kj * tk <= qi * tq + (tq - 1) + q_off
    crosses = kj * tk + (tk - 1) > qi * tq + q_off

    @pl.when(active & crosses)
    def _():
        update(True)

    @pl.when(active & jnp.logical_not(crosses))
    def _():
        update(False)

    @pl.when(kj == pl.num_programs(3) - 1)
    def _():
        for h in heads:
            o_ref[0, :, sls[h]] = (acc_scr[h] / l_scr[h]).astype(o_ref.dtype)


def _fox(q_arr, q_blk, k_arr, k_blk, v_arr, v_blk, cq_col, ck_row, *, bh, hd, f_lane):
    b, t_q, _ = q_arr.shape
    t_k = k_arr.shape[1]
    q_off = t_k - t_q
    tq = _tile(t_q, 512, 8)
    tk = _tile(t_k, 512)
    hg = 4 if bh % 4 == 0 else bh
    assert q_blk % hg == 0 and k_blk % hg == 0 and v_blk % hg == 0
    wg = hg * hd
    last = lambda qi: (qi * tq + (tq - 1) + q_off) // tk
    return pl.pallas_call(
        functools.partial(_fox_kernel, q_off, f_lane, hg, hd),
        grid=(b, bh // hg, t_q // tq, t_k // tk),
        in_specs=[
            pl.BlockSpec((1, tq, wg), lambda i, g, qi, kj: (i, qi, q_blk // hg + g)),
            pl.BlockSpec((1, tk, wg), lambda i, g, qi, kj: (i, jnp.minimum(kj, last(qi)), k_blk // hg + g)),
            pl.BlockSpec((1, tk, wg), lambda i, g, qi, kj: (i, jnp.minimum(kj, last(qi)), v_blk // hg + g)),
            pl.BlockSpec((1, tq, LANE), lambda i, g, qi, kj: (i, qi, 0)),
            pl.BlockSpec((1, LANE, tk), lambda i, g, qi, kj: (i, 0, jnp.minimum(kj, last(qi)))),
        ],
        out_specs=pl.BlockSpec((1, tq, wg), lambda i, g, qi, kj: (i, qi, g)),
        out_shape=jax.ShapeDtypeStruct((b, t_q, bh * hd), BF16),
        scratch_shapes=[
            pltpu.VMEM((hg, tq, 1), F32),
            pltpu.VMEM((hg, tq, 1), F32),
            pltpu.VMEM((hg, tq, hd), F32),
        ],
        compiler_params=_params(("parallel", "parallel", "parallel", "arbitrary")),
        name="fox",
    )(q_arr, k_arr, v_arr, cq_col, ck_row)


def _fox_t_kernel(f_lane, hg, hd, k_ref, qt_ref, vt_ref, cq_ref, ck_ref, o_ref, m_scr, l_scr, acc_scr):
    g = pl.program_id(1)
    qi = pl.program_id(2)
    kj = pl.program_id(3)
    tk = k_ref.shape[1]
    tq = qt_ref.shape[1]
    heads = range(hg)
    sls = [slice(h * hd, (h + 1) * hd) for h in heads]
    k_scale = (hd ** -0.5) * LOG2E

    @pl.when(kj == 0)
    def _():
        m_scr[...] = jnp.full(m_scr.shape, NEG_INF, F32)
        l_scr[...] = jnp.zeros(l_scr.shape, F32)
        acc_scr[...] = jnp.zeros(acc_scr.shape, F32)

    def update(masked):
        ckb = ck_ref[0] * LOG2E
        lane = lax.broadcasted_iota(jnp.int32, ckb.shape, 1)
        ck = [jnp.sum(jnp.where(lane == f_lane + g * hg + h, ckb, 0.0), axis=1, keepdims=True)
              for h in heads]
        cq = [cq_ref[0, pl.ds(f_lane + g * hg + h, 1), :] * LOG2E for h in heads]
        kb = [(k_ref[0, :, sl] * k_scale).astype(BF16) for sl in sls]
        x = [_dot(a, qt_ref[sl, :]) - c for a, sl, c in zip(kb, sls, ck)]
        if masked:
            kpos = kj * tk + lax.broadcasted_iota(jnp.int32, (tk, tq), 0)
            qpos = qi * tq + lax.broadcasted_iota(jnp.int32, (tk, tq), 1)
            keep = kpos <= qpos
            x = [jnp.where(keep, a, NEG_INF) for a in x]
        m_old = [m_scr[h] for h in heads]
        m_new = [jnp.maximum(mo, c + jnp.max(a, axis=0, keepdims=True)) for mo, c, a in zip(m_old, cq, x)]
        p = [jnp.exp2(a + (c - mn)) for a, c, mn in zip(x, cq, m_new)]
        alpha = [jnp.exp2(mo - mn) for mo, mn in zip(m_old, m_new)]
        pv = [_dot(vt_ref[sl, :], a.astype(BF16)) for a, sl in zip(p, sls)]
        for h in heads:
            l_scr[h] = alpha[h] * l_scr[h] + jnp.sum(p[h], axis=0, keepdims=True)
            acc_scr[h] = alpha[h] * acc_scr[h] + pv[h]
            m_scr[h] = m_new[h]

    active = kj * tk <= qi * tq + (tq - 1)
    crosses = kj * tk + (tk - 1) > qi * tq

    @pl.when(active & crosses)
    def _():
        update(True)

    @pl.when(active & jnp.logical_not(crosses))
    def _():
        update(False)

    @pl.when(kj == pl.num_programs(3) - 1)
    def _():
        for h in heads:
            o_ref[0, :, sls[h]] = (acc_scr[h] / l_scr[h]).T.astype(o_ref.dtype)


def _fox_t(k3, qv_t, cq_row, ck_col, *, bh, hd, f_lane):
    b, t_len, bw = k3.shape
    tq = _tile(t_len, 512)
    tk = tq
    nq = t_len // tq
    hg = 4 if bh % 4 == 0 else bh
    wg = hg * hd
    ng = bh // hg
    last = lambda qi: (qi * tq + (tq - 1)) // tk
    kblk = lambda qi, kj: jnp.minimum(kj, last(qi))
    return pl.pallas_call(
        functools.partial(_fox_t_kernel, f_lane, hg, hd),
        grid=(b, ng, nq, nq),
        in_specs=[
            pl.BlockSpec((1, tk, wg), lambda i, g, qi, kj: (i, kblk(qi, kj), g)),
            pl.BlockSpec((wg, tq), lambda i, g, qi, kj: (g, i * nq + qi)),
            pl.BlockSpec((wg, tk), lambda i, g, qi, kj: (ng + g, i * nq + kblk(qi, kj))),
            pl.BlockSpec((1, LANE, tq), lambda i, g, qi, kj: (i, 0, qi)),
            pl.BlockSpec((1, tk, LANE), lambda i, g, qi, kj: (i, kblk(qi, kj), 0)),
        ],
        out_specs=pl.BlockSpec((1, tq, wg), lambda i, g, qi, kj: (i, qi, g)),
        out_shape=jax.ShapeDtypeStruct((b, t_len, bw), BF16),
        scratch_shapes=[
            pltpu.VMEM((hg, 1, tq), F32),
            pltpu.VMEM((hg, 1, tq), F32),
            pltpu.VMEM((hg, hd, tq), F32),
        ],
        compiler_params=_params(("parallel", "parallel", "parallel", "arbitrary")),
        name="fox_t",
    )(k3, qv_t, qv_t, cq_row, ck_col)


def _layer_norm_rows(x, g, b):
    mu = jnp.mean(x, axis=1, keepdims=True)
    xc = x - mu
    var = jnp.mean(xc * xc, axis=1, keepdims=True)
    return xc * lax.rsqrt(var + LN_EPS) * g + b


def _layer_norm_ref(ref, g_ref, b_ref, rows):
    def body(r, carry):
        sl = pl.ds(pl.multiple_of(r * rows, rows), rows)
        ref[sl, :] = _layer_norm_rows(ref[sl, :], g_ref[...], b_ref[...])
        return carry
    lax.fori_loop(0, ref.shape[0] // rows, body, 0)


def _outproj_kernel(alpha, tn, oa_ref, ob_ref, wa_ref, wb_ref, x_ref, g_ref, b_ref, hid_ref, hidt_ref):
    j = pl.program_id(1)
    col = pl.multiple_of(j * tn, LANE)
    hid_ref[:, pl.ds(col, tn)] = (alpha * x_ref[...] + _dot(oa_ref[...], wa_ref[...])
                                  + _dot(ob_ref[...], wb_ref[...]))

    @pl.when(j == pl.num_programs(1) - 1)
    def _():
        tm, d = hid_ref.shape
        _layer_norm_ref(hid_ref, g_ref, b_ref, min(tm, 32))
        rb = min(tm, LANE)
        step = _tile(d, 512)
        for r in range(tm // rb):
            for c in range(d // step):
                hidt_ref[c * step:(c + 1) * step, r * rb:(r + 1) * rb] = (
                    hid_ref[r * rb:(r + 1) * rb, c * step:(c + 1) * step].T.astype(BF16))


def _outproj(o_a, o_b, w_a, w_b, x2d, g, b, alpha):
    m, d = x2d.shape
    tm = _tile(m, 512)
    tn = _tile(d, 512)
    return pl.pallas_call(
        functools.partial(_outproj_kernel, alpha, tn),
        grid=(m // tm, d // tn),
        in_specs=[
            pl.BlockSpec((tm, o_a.shape[1]), lambda i, j: (i, 0)),
            pl.BlockSpec((tm, o_b.shape[1]), lambda i, j: (i, 0)),
            pl.BlockSpec((w_a.shape[0], tn), lambda i, j: (0, j)),
            pl.BlockSpec((w_b.shape[0], tn), lambda i, j: (0, j)),
            pl.BlockSpec((tm, tn), lambda i, j: (i, j)),
            pl.BlockSpec((1, d), lambda i, j: (0, 0)),
            pl.BlockSpec((1, d), lambda i, j: (0, 0)),
        ],
        out_specs=[
            pl.BlockSpec((tm, d), lambda i, j: (i, 0)),
            pl.BlockSpec((d, tm), lambda i, j: (0, i)),
        ],
        out_shape=[
            jax.ShapeDtypeStruct((m, d), F32),
            jax.ShapeDtypeStruct((d, m), BF16),
        ],
        compiler_params=_params(("parallel", "arbitrary")),
        name="outproj",
    )(o_a, o_b, w_a, w_b, x2d, g, b)


def _top_values(x, n):
    vals = []
    for _ in range(n):
        m = jnp.max(x, axis=0, keepdims=True)
        vals.append(m)
        x = jnp.where(x == m, NEG_INF, x)
    return vals


def _route_kernel(topk, ht_ref, wq_ref, key_ref, a1_ref, a2_ref, thr_ref, cand_scr):
    dk = key_ref.shape[3]
    n = topk + 1
    qt = _dot(wq_ref[...], ht_ref[...])
    s1 = _dot(key_ref[0, 0], qt[:dk].astype(BF16))
    s2 = _dot(key_ref[0, 1], qt[dk:].astype(BF16))
    top1 = _top_values(s1, n)
    top2 = _top_values(s2, n)
    cand_scr[...] = jnp.full(cand_scr.shape, NEG_INF, F32)
    pairs = [(a, b) for a in range(n) for b in range(n) if (a + 1) * (b + 1) <= n]
    for r, (a, b) in enumerate(pairs):
        cand_scr[r:r + 1, :] = top1[a] + top2[b]
    best = _top_values(cand_scr[...], n)
    z = jnp.exp(best[0] - best[0])
    for t in best[1:topk]:
        z = z + jnp.exp(t - best[0])
    a1_ref[0] = jnp.exp(s1 - top1[0]) / z
    a2_ref[0] = jnp.exp(s2 - top2[0])
    thr = jnp.exp(0.5 * (best[topk - 1] + best[topk]) - best[0]) / z
    thr_ref[0] = jnp.broadcast_to(thr, thr_ref.shape[1:])


def _num_candidates(n):
    return -(-sum(n // (a + 1) for a in range(n)) // 8) * 8


def _route(hid_t, wq_t, keys, topk):
    d, m = hid_t.shape
    ph, _, nk, dk = keys.shape
    tm = _tile(m, 512)
    out = jax.ShapeDtypeStruct((ph, nk, m), F32)
    blk = pl.BlockSpec((1, nk, tm), lambda i, h: (h, 0, i))
    return pl.pallas_call(
        functools.partial(_route_kernel, topk),
        grid=(m // tm, ph),
        in_specs=[
            pl.BlockSpec((d, tm), lambda i, h: (0, i)),
            pl.BlockSpec((2 * dk, d), lambda i, h: (h, 0)),
            pl.BlockSpec((1, 2, nk, dk), lambda i, h: (h, 0, 0, 0)),
        ],
        out_specs=[blk, blk, pl.BlockSpec((1, 8, tm), lambda i, h: (h, 0, i))],
        out_shape=[out, out, jax.ShapeDtypeStruct((ph, 8, m), F32)],
        scratch_shapes=[pltpu.VMEM((_num_candidates(topk + 1), tm), F32)],
        compiler_params=_params(("parallel", "arbitrary")),
        name="peer_route",
    )(hid_t, wq_t, keys)


def _peer_kernel(ht_ref, u_ref, vt_ref, a1_ref, a2_ref, thr_ref, o_ref, act_scr, pre_scr):
    e = pl.program_id(1)
    n_tiles = pl.num_programs(1) - 1
    ph, nk, _ = a2_ref.shape
    te = u_ref.shape[0]
    n_sub = te // nk
    d = o_ref.shape[0]
    step = d // n_sub
    slot = e % 2

    @pl.when(e == 0)
    def _():
        o_ref[...] = jnp.zeros(o_ref.shape, F32)
        act_scr[1] = jnp.zeros(act_scr.shape[1:], BF16)

    @pl.when(e < n_tiles)
    def _():
        pre_scr[...] = _dot(u_ref[...], ht_ref[...])

    def body(c, carry):
        sl = pl.ds(pl.multiple_of(c * step, step), step)
        o_ref[sl, :] += _dot(vt_ref[sl, :], act_scr[1 - slot])
        row = jnp.minimum(e * n_sub + c, nk - 1)
        rs = pl.ds(pl.multiple_of(c * nk, nk), nk)
        a1_rows = [a1_ref[h, pl.ds(row, 1), :] for h in range(ph)]
        tm = a2_ref.shape[2]
        strip = min(tm, LANE)
        for t in range(tm // strip):
            ls = slice(t * strip, (t + 1) * strip)
            gate = None
            for h in range(ph):
                g = a2_ref[h, :, ls] * a1_rows[h][:, ls]
                term = jnp.where(g >= thr_ref[h, 0:1, ls], g, 0.0)
                gate = term if gate is None else gate + term
            x = pre_scr[rs, ls]
            act = gate * (0.5 * x * (1.0 + lax.erf(x * (2.0 ** -0.5))))
            act_scr[slot, rs, ls] = act.astype(BF16)
        return carry
    lax.fori_loop(0, n_sub, body, 0, unroll=True)


def _peer(hid_t, u_b, v_t, a1, a2, thr):
    d, m = hid_t.shape
    ne = u_b.shape[0]
    ph, nk, _ = a1.shape
    tm = _tile(m, 512)
    te = _tile(ne, 512)
    n_tiles = ne // te
    assert te % nk == 0 and d % (te // nk) == 0
    rt = pl.BlockSpec((ph, nk, tm), lambda i, e: (0, 0, i))
    return pl.pallas_call(
        _peer_kernel,
        grid=(m // tm, n_tiles + 1),
        in_specs=[
            pl.BlockSpec((d, tm), lambda i, e: (0, i)),
            pl.BlockSpec((te, d), lambda i, e: (jnp.minimum(e, n_tiles - 1), 0)),
            pl.BlockSpec((d, te), lambda i, e: (0, jnp.maximum(e - 1, 0))),
            rt, rt,
            pl.BlockSpec((ph, 8, tm), lambda i, e: (0, 0, i)),
        ],
        out_specs=pl.BlockSpec((d, tm), lambda i, e: (0, i)),
        out_shape=jax.ShapeDtypeStruct((d, m), F32),
        scratch_shapes=[pltpu.VMEM((2, te, tm), BF16), pltpu.VMEM((te, tm), F32)],
        compiler_params=_params(("parallel", "arbitrary")),
        name="peer_dense",
    )(hid_t, u_b, v_t, a1, a2, thr)


def _final_kernel(alpha, hid_ref, pt_ref, g_ref, b_ref, y_ref):
    d = hid_ref.shape[1]
    step = _tile(d, 512)
    for c in range(d // step):
        sl = slice(c * step, (c + 1) * step)
        y_ref[:, sl] = alpha * hid_ref[:, sl] + pt_ref[sl, :].T
    _layer_norm_ref(y_ref, g_ref, b_ref, min(y_ref.shape[0], 32))


def _final(hid, peer_t, g, b, alpha):
    m, d = hid.shape
    tm = _tile(m, 256)
    return pl.pallas_call(
        functools.partial(_final_kernel, alpha),
        grid=(m // tm,),
        in_specs=[
            pl.BlockSpec((tm, d), lambda i: (i, 0)),
            pl.BlockSpec((d, tm), lambda i: (0, i)),
            pl.BlockSpec((1, d), lambda i: (0, 0)),
            pl.BlockSpec((1, d), lambda i: (0, 0)),
        ],
        out_specs=pl.BlockSpec((tm, d), lambda i: (i, 0)),
        out_shape=jax.ShapeDtypeStruct((m, d), F32),
        compiler_params=_params(("parallel",)),
        name="final_ln",
    )(hid, peer_t, g, b)


def _pad_lanes(a):
    return jnp.pad(a, [(0, 0)] * (a.ndim - 1) + [(0, LANE - a.shape[-1])])


def _trunk(x, conv_hist, s0, fox_cache, wts, depth):
    (w_main, w_small, conv_w, prow, pcol, norm_w, w_a, w_b, ln1_g, ln1_b, wq_t, keys, u_b, v_t,
     ln2_g, ln2_b, ah, bh, hd, topk) = wts
    b, t_len, d = x.shape
    m = b * t_len
    aw, bw = ah * hd, bh * hd
    alpha = (2 * depth) ** 0.25
    f_lane = 2 * ah
    x2d = x.reshape(m, d)

    n_a = 4 * aw
    prompt = fox_cache is None
    proj, k_new, v_new, sm, smt, *qv_t = _inproj(x2d, w_main, w_small, n_a, bw, prompt)
    proj3 = proj.reshape(b, t_len, -1)
    k_new = k_new.reshape(b, t_len, bw)
    v_new = v_new.reshape(b, t_len, bw)

    c_len = min(64, t_len)
    zero_c = jnp.zeros((b, 1, LANE), F32)
    zero_r = jnp.zeros((b, LANE, 1), F32)
    if fox_cache is None:
        carry_c, carry_r = zero_c, zero_r
    else:
        clf = fox_cache[2].astype(F32)
        p_len = clf.shape[1]
        clf_col = jnp.pad(clf, ((0, 0), (0, 0), (f_lane, LANE - f_lane - bh)))
        clf_row = jnp.swapaxes(clf_col, 1, 2)
        cc_col, _, cc_row = _gates(clf_col, clf_row, zero_c, zero_r, prow, pcol,
                                   apply=False, c_len=min(64, p_len), ah=ah, bh=bh)
        carry_c = cc_col[:, p_len - 1:, :]
        carry_r = cc_row[:, :, p_len - 1:]
    smt3 = jnp.swapaxes(smt.reshape(LANE, b, t_len), 0, 1)
    col, rowc, rowf = _gates(sm.reshape(b, t_len, LANE), smt3, carry_c, carry_r, prow, pcol,
                             apply=True, c_len=c_len, ah=ah, bh=bh)
    logf = col[:, :, f_lane + bh:f_lane + 2 * bh]

    o_a, s_new = _gdn(proj3, conv_hist, conv_w, col, rowc, s0, norm_w, c_len=c_len, ah=ah, hd=hd)
    conv_new = jnp.concatenate([conv_hist.astype(F32), proj3[:, :, :3 * aw]], axis=1)[:, -3:]

    qb0 = n_a // hd
    if prompt:
        o_b = _fox_t(k_new, qv_t[0], rowf, col, bh=bh, hd=hd, f_lane=f_lane)
    else:
        k_all = jnp.concatenate([fox_cache[0].reshape(b, -1, bw).astype(F32), k_new], axis=1)
        v_all = jnp.concatenate([fox_cache[1].reshape(b, -1, bw).astype(F32), v_new], axis=1)
        ck_row = jnp.concatenate([cc_row, rowf], axis=2)
        o_b = _fox(proj3, qb0, k_all, 0, v_all, 0, col, ck_row, bh=bh, hd=hd, f_lane=f_lane)

    hid, hid_t = _outproj(o_a.reshape(m, aw), o_b.reshape(m, bw), w_a, w_b, x2d, ln1_g, ln1_b, alpha)
    a1, a2, thr = _route(hid_t, wq_t, keys, topk)
    peer_t = _peer(hid_t, u_b, v_t, a1, a2, thr)
    y = _final(hid, peer_t, ln2_g, ln2_b, alpha).reshape(b, t_len, d)
    return y, (k_new.reshape(b, t_len, bh, hd), v_new.reshape(b, t_len, bh, hd), logf, s_new, conv_new)


def kernel(x_prompt, x_sample, cache_fox_k, cache_fox_v, cache_fox_logf, state_gdn, state_gdn_conv,
           w_in, gdn_conv_w, gdn_a_log, gdn_dt_bias, gdn_norm_w, fox_f_bias, w_out, ln1_g, ln1_b,
           peer_w_q, peer_sub_keys, peer_u, peer_v, ln2_g, ln2_b):
    depth = w_in.shape[0]
    ah = gdn_a_log.shape[1]
    bh = fox_f_bias.shape[1]
    hd = gdn_norm_w.shape[1]
    aw, bw = ah * hd, bh * hd
    topk = 16
    assert 2 * ah + 2 * bh <= LANE
    n_p = x_prompt.shape[0]
    yp, ys = x_prompt, x_sample
    outs_p, outs_s = [], []
    for l in range(depth):
        o_a_a = 4 * aw
        o_b_qkv = o_a_a + 2 * ah
        o_b_f = o_b_qkv + 3 * bw
        wl = w_in[l]
        w_main = jnp.concatenate([wl[:, :o_a_a], wl[:, o_b_qkv:o_b_f]], axis=1).astype(BF16)
        w_f = wl[:, o_b_f:o_b_f + bh]
        w_small = _pad_lanes(jnp.concatenate([wl[:, o_a_a:o_b_qkv], w_f, w_f], axis=1)).astype(BF16)
        zeros_a = jnp.zeros((ah,), F32)
        prow = jnp.stack([
            _pad_lanes(gdn_a_log[l].astype(F32)),
            _pad_lanes(jnp.concatenate([gdn_dt_bias[l].astype(F32), zeros_a,
                                        fox_f_bias[l].astype(F32), fox_f_bias[l].astype(F32)])),
        ])
        wts = (w_main, w_small, gdn_conv_w[l].astype(F32), prow, prow.T,
               gdn_norm_w[l].reshape(1, hd).astype(F32),
               w_out[l][:aw].astype(BF16), w_out[l][aw:].astype(BF16),
               ln1_g[l].reshape(1, -1), ln1_b[l].reshape(1, -1),
               peer_w_q[l].T.astype(BF16), peer_sub_keys[l].astype(BF16),
               peer_u[l].astype(BF16), peer_v[l].T.astype(BF16),
               ln2_g[l].reshape(1, -1), ln2_b[l].reshape(1, -1), ah, bh, hd, topk)
        conv0 = jnp.zeros((n_p, 3, 3 * aw), yp.dtype)
        s0 = jnp.zeros((n_p, ah, hd, hd), yp.dtype)
        yp, st_p = _trunk(yp, conv0, s0, None, wts, depth)
        ys, st_s = _trunk(ys, state_gdn_conv[l], state_gdn[l],
                          (cache_fox_k[l], cache_fox_v[l], cache_fox_logf[l]), wts, depth)
        outs_p.append(st_p)
        outs_s.append(st_s)
    stack = lambda outs, n: jnp.stack([o[n] for o in outs], axis=0)
    return ((yp, ys) + tuple(stack(outs_p, n) for n in range(5))
            + tuple(stack(outs_s, n) for n in range(5)))
```

```python
import functools

import jax
import jax.numpy as jnp
from jax import lax
from jax.experimental import pallas as pl
from jax.experimental.pallas import tpu as pltpu

F32 = jnp.float32
BF16 = jnp.bfloat16
LANE = 128
LN_EPS = 1e-5
RMS_EPS = 1e-6
L2_EPS = 1e-6
CONV_PAD = 8
VMEM_LIMIT = 58 * 1024 * 1024
HIGHEST = lax.Precision.HIGHEST
NEG_INF = float("-inf")
LOG2E = 1.4426950408889634


def _params(sem):
    return pltpu.CompilerParams(dimension_semantics=sem, vmem_limit_bytes=VMEM_LIMIT)


def _tile(n, pref, align=LANE):
    if n <= pref:
        return n
    t = (pref // align) * align
    while t >= align:
        if n % t == 0:
            return t
        t -= align
    return n


def _dot(a, b):
    return jnp.dot(a, b, preferred_element_type=F32)


def _dot_nt(a, b):
    return lax.dot_general(a, b, (((1,), (1,)), ((), ())), preferred_element_type=F32)


def _dot_tn(a, b):
    return lax.dot_general(a, b, (((0,), (0,)), ((), ())), preferred_element_type=F32)


def _sigmoid(x):
    return 1.0 / (1.0 + jnp.exp(-x))


def _softplus(x):
    return jnp.maximum(x, 0.0) + jnp.log1p(jnp.exp(-jnp.abs(x)))


def _silu(x):
    return x * _sigmoid(x)


def _inproj_kernel(j_q, j_k, j_v, emit_t, x_ref, w_ref, ws_ref, o_ref, k_ref, v_ref, s_ref, st_ref, *rest):
    xb_ref = rest[-1]
    j = pl.program_id(1)

    @pl.when(j == 0)
    def _():
        xb = x_ref[...].astype(BF16)
        xb_ref[...] = xb
        sm = _dot(xb, ws_ref[...])
        s_ref[...] = sm
        st_ref[...] = sm.T

    acc = _dot(xb_ref[...], w_ref[...])

    @pl.when(j < j_k)
    def _():
        o_ref[...] = acc

    @pl.when((j >= j_k) & (j < j_v))
    def _():
        k_ref[...] = acc

    @pl.when(j >= j_v)
    def _():
        v_ref[...] = acc

    if emit_t:
        @pl.when(((j >= j_q) & (j < j_k)) | (j >= j_v))
        def _():
            rest[0][...] = acc.T.astype(BF16)


def _inproj(x2d, w_main, w_small, n_a, bw, emit_t):
    m, d = x2d.shape
    n = w_main.shape[1]
    assert n == n_a + 3 * bw
    tm = _tile(m, 512)
    tn = _tile(bw, 1024)
    assert n_a % tn == 0 and bw % tn == 0
    nb = bw // tn
    j_q = n_a // tn
    j_k = j_q + nb
    j_v = j_k + nb
    out_specs = [
        pl.BlockSpec((tm, tn), lambda i, j: (i, jnp.minimum(j, j_k - 1))),
        pl.BlockSpec((tm, tn), lambda i, j: (i, jnp.clip(j - j_k, 0, nb - 1))),
        pl.BlockSpec((tm, tn), lambda i, j: (i, jnp.maximum(j - j_v, 0))),
        pl.BlockSpec((tm, LANE), lambda i, j: (i, 0)),
        pl.BlockSpec((LANE, tm), lambda i, j: (0, i)),
    ]
    out_shape = [
        jax.ShapeDtypeStruct((m, n_a + bw), F32),
        jax.ShapeDtypeStruct((m, bw), F32),
        jax.ShapeDtypeStruct((m, bw), F32),
        jax.ShapeDtypeStruct((m, LANE), F32),
        jax.ShapeDtypeStruct((LANE, m), F32),
    ]
    if emit_t:
        row_blk = lambda j: jnp.where(j < j_k, jnp.maximum(j - j_q, 0),
                                      jnp.where(j < j_v, nb - 1, j - j_v + nb))
        out_specs.append(pl.BlockSpec((tn, tm), lambda i, j: (row_blk(j), i)))
        out_shape.append(jax.ShapeDtypeStruct((2 * bw, m), BF16))
    return pl.pallas_call(
        functools.partial(_inproj_kernel, j_q, j_k, j_v, emit_t),
        grid=(m // tm, n // tn),
        in_specs=[
            pl.BlockSpec((tm, d), lambda i, j: (i, 0)),
            pl.BlockSpec((d, tn), lambda i, j: (0, j)),
            pl.BlockSpec((d, LANE), lambda i, j: (0, 0)),
        ],
        out_specs=out_specs,
        out_shape=out_shape,
        scratch_shapes=[pltpu.VMEM((tm, d), BF16)],
        compiler_params=_params(("parallel", "arbitrary")),
        name="inproj",
    )(x2d, w_main, w_small)


def _gate_values(z, a_log, bias, idx, ah, bh):
    zz = z + bias
    g = -jnp.exp(a_log) * _softplus(zz)
    beta = _sigmoid(zz)
    logf = -_softplus(-zz)
    return jnp.where(idx < ah, g, jnp.where(idx < 2 * ah, beta, jnp.where(idx < 2 * ah + 2 * bh, logf, 0.0)))


def _gate_merge(idx, cs, y, carry, ah, bh):
    return jnp.where(idx < ah, cs,
                     jnp.where(idx < 2 * ah, y,
                               jnp.where(idx < 2 * ah + bh, cs + carry,
                                         jnp.where(idx < 2 * ah + 2 * bh, y, 0.0))))


def _gates_kernel(apply, c_len, ah, bh, sm_ref, smt_ref, cc_ref, cr_ref, prow_ref, pcol_ref,
                  col_ref, rowc_ref, rowf_ref):
    t_len = sm_ref.shape[1]
    nc = t_len // c_len
    ii = lax.broadcasted_iota(jnp.int32, (c_len, c_len), 0)
    jj = lax.broadcasted_iota(jnp.int32, (c_len, c_len), 1)
    tril = (ii >= jj).astype(F32)
    triu = (ii <= jj).astype(F32)
    lane = lax.broadcasted_iota(jnp.int32, (c_len, LANE), 1)
    subl = lax.broadcasted_iota(jnp.int32, (LANE, c_len), 0)
    carry_c = cc_ref[0]
    carry_r = cr_ref[0]
    for c in range(nc):
        z = sm_ref[0, c * c_len:(c + 1) * c_len, :]
        y = _gate_values(z, prow_ref[0:1, :], prow_ref[1:2, :], lane, ah, bh) if apply else z
        cs = jnp.dot(tril, y, precision=HIGHEST, preferred_element_type=F32)
        col_ref[0, c * c_len:(c + 1) * c_len, :] = _gate_merge(lane, cs, y, carry_c, ah, bh)
        carry_c = carry_c + cs[c_len - 1:c_len, :]

        zt = smt_ref[0, :, c * c_len:(c + 1) * c_len]
        yt = _gate_values(zt, pcol_ref[:, 0:1], pcol_ref[:, 1:2], subl, ah, bh) if apply else zt
        cst = jnp.dot(yt, triu, precision=HIGHEST, preferred_element_type=F32)
        out_t = _gate_merge(subl, cst, yt, carry_r, ah, bh)
        rowc_ref[0, c] = out_t
        rowf_ref[0, :, c * c_len:(c + 1) * c_len] = out_t
        carry_r = carry_r + cst[:, c_len - 1:c_len]


def _gates(sm3, smt, carry_col, carry_row, prow, pcol, *, apply, c_len, ah, bh):
    b, t_len, _ = sm3.shape
    nc = t_len // c_len
    return pl.pallas_call(
        functools.partial(_gates_kernel, apply, c_len, ah, bh),
        grid=(b,),
        in_specs=[
            pl.BlockSpec((1, t_len, LANE), lambda i: (i, 0, 0)),
            pl.BlockSpec((1, LANE, t_len), lambda i: (i, 0, 0)),
            pl.BlockSpec((1, 1, LANE), lambda i: (i, 0, 0)),
            pl.BlockSpec((1, LANE, 1), lambda i: (i, 0, 0)),
            pl.BlockSpec((2, LANE), lambda i: (0, 0)),
            pl.BlockSpec((LANE, 2), lambda i: (0, 0)),
        ],
        out_specs=[
            pl.BlockSpec((1, t_len, LANE), lambda i: (i, 0, 0)),
            pl.BlockSpec((1, nc, LANE, c_len), lambda i: (i, 0, 0, 0)),
            pl.BlockSpec((1, LANE, t_len), lambda i: (i, 0, 0)),
        ],
        out_shape=[
            jax.ShapeDtypeStruct((b, t_len, LANE), F32),
            jax.ShapeDtypeStruct((b, nc, LANE, c_len), F32),
            jax.ShapeDtypeStruct((b, LANE, t_len), F32),
        ],
        compiler_params=_params(("parallel",)),
        name="gates",
    )(sm3, smt, carry_col, carry_row, prow, pcol)


def _unit_lower_inverses(lows, eye, ii, jj):
    c_len = lows[0].shape[0]
    same0 = (ii >> 1) == (jj >> 1)
    ts = [eye - jnp.where(same0, low, 0.0) for low in lows]
    k = 1
    while (2 << k) <= c_len:
        sel = ((ii >> (k + 1)) == (jj >> (k + 1))) & (((ii >> k) & 1) == 1) & (((jj >> k) & 1) == 0)
        tbs = [t.astype(BF16) for t in ts]
        mid = [_dot(tb, jnp.where(sel, low, 0.0).astype(BF16)).astype(BF16) for tb, low in zip(tbs, lows)]
        ts = [t - _dot(m, tb) for t, m, tb in zip(ts, mid, tbs)]
        k += 1
    return ts


def _gdn_kernel(c_len, hb, ah, q_ref, k_ref, v_ref, z_ref, hq_ref, hk_ref, hv_ref,
                cq_ref, ck_ref, cv_ref, col_ref, rowc_ref, s0_ref, nw_ref,
                o_ref, sn_ref, s_scr, buf_scr):
    c_idx = pl.program_id(2)
    hd = nw_ref.shape[1]
    lo = CONV_PAD - 3

    @pl.when(c_idx == 0)
    def _():
        s_scr[...] = s0_ref[0]
        buf_scr[0, lo:CONV_PAD, :] = hq_ref[0]
        buf_scr[1, lo:CONV_PAD, :] = hk_ref[0]
        buf_scr[2, lo:CONV_PAD, :] = hv_ref[0]

    conv = []
    for n, (x_ref, w_ref) in enumerate(((q_ref, cq_ref), (k_ref, ck_ref), (v_ref, cv_ref))):
        buf_scr[n, CONV_PAD:CONV_PAD + c_len, :] = x_ref[0]
        acc = w_ref[0:1, :] * buf_scr[n, lo:lo + c_len, :]
        for w in range(1, 4):
            acc = acc + w_ref[w:w + 1, :] * buf_scr[n, lo + w:lo + w + c_len, :]
        buf_scr[n, lo:CONV_PAD, :] = buf_scr[n, lo + c_len:CONV_PAD + c_len, :]
        conv.append(_silu(acc))
    qc, kc, vc = conv

    colblk = col_ref[0]
    ii = lax.broadcasted_iota(jnp.int32, (c_len, c_len), 0)
    jj = lax.broadcasted_iota(jnp.int32, (c_len, c_len), 1)
    eye = (ii == jj).astype(F32)
    nw = nw_ref[...]
    heads = range(hb)
    sls = [slice(h * hd, (h + 1) * hd) for h in heads]

    cum_c = [colblk[:, h:h + 1] for h in heads]
    beta_c = [colblk[:, ah + h:ah + h + 1] for h in heads]
    cum_r = [rowc_ref[0, 0, h:h + 1, :] for h in heads]
    cum_last = [r[:, c_len - 1:c_len] for r in cum_r]
    e_cum = [jnp.exp(c) for c in cum_c]

    q = [qc[:, sl] for sl in sls]
    k = [kc[:, sl] for sl in sls]
    q = [x * (lax.rsqrt(jnp.sum(x * x, axis=1, keepdims=True) + L2_EPS) * (hd ** -0.5)) for x in q]
    k = [x * lax.rsqrt(jnp.sum(x * x, axis=1, keepdims=True) + L2_EPS) for x in k]
    qb = [x.astype(BF16) for x in q]
    kb = [x.astype(BF16) for x in k]

    decay = [jnp.exp(jnp.where(ii >= jj, c - r, NEG_INF)) for c, r in zip(cum_c, cum_r)]
    kk = [_dot_nt(x, x) for x in kb]
    qk = [_dot_nt(x, y) for x, y in zip(qb, kb)]
    lows = [jnp.where(ii > jj, b * m * d, 0.0) for b, m, d in zip(beta_c, kk, decay)]
    t_inv = _unit_lower_inverses(lows, eye, ii, jj)

    rhs = [jnp.concatenate([b * vc[:, sl], (b * e) * x], axis=1).astype(BF16)
           for b, e, x, sl in zip(beta_c, e_cum, k, sls)]
    w_all = [_dot(t.astype(BF16), r) for t, r in zip(t_inv, rhs)]

    s_old = [s_scr[h] for h in heads]
    sb = [s.astype(BF16) for s in s_old]
    u = [w[:, :hd] - _dot(w[:, hd:].astype(BF16), s) for w, s in zip(w_all, sb)]
    ub = [x.astype(BF16) for x in u]
    o = [_dot((x * e).astype(BF16), s) + _dot((m * d).astype(BF16), y)
         for x, e, s, m, d, y in zip(q, e_cum, sb, qk, decay, ub)]
    for h in heads:
        k_dec = k[h] * jnp.exp(cum_last[h] - cum_c[h])
        s_scr[h] = s_old[h] * jnp.exp(cum_last[h]) + _dot_tn(k_dec.astype(BF16), ub[h])
    for h in heads:
        x = o[h] * lax.rsqrt(jnp.mean(o[h] * o[h], axis=1, keepdims=True) + RMS_EPS) * nw
        o_ref[0, :, sls[h]] = (x * _silu(z_ref[0, :, sls[h]])).astype(o_ref.dtype)

    @pl.when(c_idx == pl.num_programs(2) - 1)
    def _():
        sn_ref[0] = s_scr[...]


def _gdn(proj3, conv_hist, conv_w, col, rowc, s0, norm_w, *, c_len, ah, hd):
    b, t_len, _ = proj3.shape
    nc = t_len // c_len
    aw = ah * hd
    hb = ah
    ng = ah // hb
    wb = hb * hd
    kq, kk_, kv, kz = 0, ng, 2 * ng, 3 * ng
    tok = lambda off: pl.BlockSpec((1, c_len, wb), lambda i, g, c: (i, c, off + g))
    hist = lambda off: pl.BlockSpec((1, 3, wb), lambda i, g, c: (i, 0, off + g))
    cw = lambda off: pl.BlockSpec((4, wb), lambda i, g, c: (0, off + g))
    return pl.pallas_call(
        functools.partial(_gdn_kernel, c_len, hb, ah),
        grid=(b, ng, nc),
        in_specs=[
            tok(kq), tok(kk_), tok(kv), tok(kz),
            hist(kq), hist(kk_), hist(kv),
            cw(kq), cw(kk_), cw(kv),
            pl.BlockSpec((1, c_len, LANE), lambda i, g, c: (i, c, 0)),
            pl.BlockSpec((1, 1, LANE, c_len), lambda i, g, c: (i, c, 0, 0)),
            pl.BlockSpec((1, hb, hd, hd), lambda i, g, c: (i, g, 0, 0)),
            pl.BlockSpec((1, hd), lambda i, g, c: (0, 0)),
        ],
        out_specs=[
            pl.BlockSpec((1, c_len, wb), lambda i, g, c: (i, c, g)),
            pl.BlockSpec((1, hb, hd, hd), lambda i, g, c: (i, g, 0, 0)),
        ],
        out_shape=[
            jax.ShapeDtypeStruct((b, t_len, aw), BF16),
            jax.ShapeDtypeStruct((b, ah, hd, hd), F32),
        ],
        scratch_shapes=[
            pltpu.VMEM((hb, hd, hd), F32),
            pltpu.VMEM((3, CONV_PAD + c_len, wb), F32),
        ],
        compiler_params=_params(("parallel", "parallel", "arbitrary")),
        name="gdn",
    )(proj3, proj3, proj3, proj3, conv_hist, conv_hist, conv_hist, conv_w, conv_w, conv_w,
      col, rowc, s0, norm_w)


def _fox_s_kernel(f_lane, bh, hd, q_ref, kn_ref, vn_ref, kc_ref, vc_ref, cq_ref, ckc_ref, ckn_ref, o_ref):
    t_q = q_ref.shape[1]
    p_len = kc_ref.shape[1] // bh
    heads = range(bh)
    sls = [slice(h * hd, (h + 1) * hd) for h in heads]
    q_scale = (hd ** -0.5) * LOG2E
    cqb = cq_ref[0] * LOG2E
    keep = (lax.broadcasted_iota(jnp.int32, (t_q, t_q), 1) <= lax.broadcasted_iota(jnp.int32, (t_q, t_q), 0))
    cq = [cqb[:, f_lane + h:f_lane + h + 1] for h in heads]
    ckc = [ckc_ref[0, f_lane + h:f_lane + h + 1, :] * LOG2E for h in heads]
    ckn = [ckn_ref[0, f_lane + h:f_lane + h + 1, :] * LOG2E for h in heads]
    qb = [(q_ref[0, :, sl] * q_scale).astype(BF16) for sl in sls]
    kc = [kc_ref[0, pl.ds(h, p_len, stride=bh), :].astype(BF16) for h in heads]
    xc = [_dot_nt(a, b) - c for a, b, c in zip(qb, kc, ckc)]
    xn = [jnp.where(keep, _dot_nt(a, kn_ref[0, :, sl].astype(BF16)) - c, NEG_INF)
          for a, sl, c in zip(qb, sls, ckn)]
    m = [c + jnp.maximum(jnp.max(a, axis=1, keepdims=True), jnp.max(b, axis=1, keepdims=True))
         for a, b, c in zip(xc, xn, cq)]
    pc = [jnp.exp2(a + (c - mm)) for a, c, mm in zip(xc, cq, m)]
    pn = [jnp.exp2(a + (c - mm)) for a, c, mm in zip(xn, cq, m)]
    vc = [vc_ref[0, pl.ds(h, p_len, stride=bh), :].astype(BF16) for h in heads]
    acc = [_dot(a.astype(BF16), b) + _dot(c.astype(BF16), vn_ref[0, :, sl].astype(BF16))
           for a, b, c, sl in zip(pc, vc, pn, sls)]
    for h in heads:
        den = jnp.sum(pc[h], axis=1, keepdims=True) + jnp.sum(pn[h], axis=1, keepdims=True)
        o_ref[0, :, sls[h]] = (acc[h] / den).astype(o_ref.dtype)


def _fox_s(q_arr, q_blk, k_new, v_new, k_cache, v_cache, cq_col, ck_cache_row, ck_new_row, *, bh, hd, f_lane):
    b, t_q, bw = k_new.shape
    rows = k_cache.shape[1]
    p_len = rows // bh
    return pl.pallas_call(
        functools.partial(_fox_s_kernel, f_lane, bh, hd),
        grid=(b,),
        in_specs=[
            pl.BlockSpec((1, t_q, bw), lambda i: (i, 0, q_blk)),
            pl.BlockSpec((1, t_q, bw), lambda i: (i, 0, 0)),
            pl.BlockSpec((1, t_q, bw), lambda i: (i, 0, 0)),
            pl.BlockSpec((1, rows, hd), lambda i: (i, 0, 0)),
            pl.BlockSpec((1, rows, hd), lambda i: (i, 0, 0)),
            pl.BlockSpec((1, t_q, LANE), lambda i: (i, 0, 0)),
            pl.BlockSpec((1, LANE, p_len), lambda i: (i, 0, 0)),
            pl.BlockSpec((1, LANE, t_q), lambda i: (i, 0, 0)),
        ],
        out_specs=pl.BlockSpec((1, t_q, bw), lambda i: (i, 0, 0)),
        out_shape=jax.ShapeDtypeStruct((b, t_q, bw), BF16),
        compiler_params=_params(("parallel",)),
        name="fox_s",
    )(q_arr, k_new, v_new, k_cache, v_cache, cq_col, ck_cache_row, ck_new_row)


def _fox_t_kernel(f_lane, hg, hd, k_ref, qt_ref, vt_ref, cq_ref, ck_ref, o_ref, m_scr, l_scr, acc_scr):
    g = pl.program_id(1)
    qi = pl.program_id(2)
    kj = pl.program_id(3)
    tk = k_ref.shape[1]
    tq = qt_ref.shape[1]
    heads = range(hg)
    sls = [slice(h * hd, (h + 1) * hd) for h in heads]
    k_scale = (hd ** -0.5) * LOG2E

    @pl.when(kj == 0)
    def _():
        m_scr[...] = jnp.full(m_scr.shape, NEG_INF, F32)
        l_scr[...] = jnp.zeros(l_scr.shape, F32)
        acc_scr[...] = jnp.zeros(acc_scr.shape, F32)

    def update(masked):
        ckb = ck_ref[0] * LOG2E
        lane = lax.broadcasted_iota(jnp.int32, ckb.shape, 1)
        ck = [jnp.sum(jnp.where(lane == f_lane + g * hg + h, ckb, 0.0), axis=1, keepdims=True)
              for h in heads]
        cq = [cq_ref[0, pl.ds(f_lane + g * hg + h, 1), :] * LOG2E for h in heads]
        kb = [(k_ref[0, :, sl] * k_scale).astype(BF16) for sl in sls]
        x = [_dot(a, qt_ref[sl, :]) - c for a, sl, c in zip(kb, sls, ck)]
        if masked:
            kpos = kj * tk + lax.broadcasted_iota(jnp.int32, (tk, tq), 0)
            qpos = qi * tq + lax.broadcasted_iota(jnp.int32, (tk, tq), 1)
            keep = kpos <= qpos
            x = [jnp.where(keep, a, NEG_INF) for a in x]
        m_old = [m_scr[h] for h in heads]
        m_new = [jnp.maximum(mo, c + jnp.max(a, axis=0, keepdims=True)) for mo, c, a in zip(m_old, cq, x)]
        p = [jnp.exp2(a + (c - mn)) for a, c, mn in zip(x, cq, m_new)]
        alpha = [jnp.exp2(mo - mn) for mo, mn in zip(m_old, m_new)]
        pv = [_dot(vt_ref[sl, :], a.astype(BF16)) for a, sl in zip(p, sls)]
        for h in heads:
            l_scr[h] = alpha[h] * l_scr[h] + jnp.sum(p[h], axis=0, keepdims=True)
            acc_scr[h] = alpha[h] * acc_scr[h] + pv[h]
            m_scr[h] = m_new[h]

    active = kj * tk <= qi * tq + (tq - 1)
    crosses = kj * tk + (tk - 1) > qi * tq

    @pl.when(active & crosses)
    def _():
        update(True)

    @pl.when(active & jnp.logical_not(crosses))
    def _():
        update(False)

    @pl.when(kj == pl.num_programs(3) - 1)
    def _():
        for h in heads:
            o_ref[0, :, sls[h]] = (acc_scr[h] / l_scr[h]).T.astype(o_ref.dtype)


def _fox_t(k3, qv_t, cq_row, ck_col, *, bh, hd, f_lane):
    b, t_len, bw = k3.shape
    tq = _tile(t_len, 512)
    tk = tq
    nq = t_len // tq
    hg = 4 if bh % 4 == 0 else bh
    wg = hg * hd
    ng = bh // hg
    last = lambda qi: (qi * tq + (tq - 1)) // tk
    kblk = lambda qi, kj: jnp.minimum(kj, last(qi))
    return pl.pallas_call(
        functools.partial(_fox_t_kernel, f_lane, hg, hd),
        grid=(b, ng, nq, nq),
        in_specs=[
            pl.BlockSpec((1, tk, wg), lambda i, g, qi, kj: (i, kblk(qi, kj), g)),
            pl.BlockSpec((wg, tq), lambda i, g, qi, kj: (g, i * nq + qi)),
            pl.BlockSpec((wg, tk), lambda i, g, qi, kj: (ng + g, i * nq + kblk(qi, kj))),
            pl.BlockSpec((1, LANE, tq), lambda i, g, qi, kj: (i, 0, qi)),
            pl.BlockSpec((1, tk, LANE), lambda i, g, qi, kj: (i, kblk(qi, kj), 0)),
        ],
        out_specs=pl.BlockSpec((1, tq, wg), lambda i, g, qi, kj: (i, qi, g)),
        out_shape=jax.ShapeDtypeStruct((b, t_len, bw), BF16),
        scratch_shapes=[
            pltpu.VMEM((hg, 1, tq), F32),
            pltpu.VMEM((hg, 1, tq), F32),
            pltpu.VMEM((hg, hd, tq), F32),
        ],
        compiler_params=_params(("parallel", "parallel", "parallel", "arbitrary")),
        name="fox_t",
    )(k3, qv_t, qv_t, cq_row, ck_col)


def _layer_norm_rows(x, g, b):
    mu = jnp.mean(x, axis=1, keepdims=True)
    xc = x - mu
    var = jnp.mean(xc * xc, axis=1, keepdims=True)
    return xc * lax.rsqrt(var + LN_EPS) * g + b


def _layer_norm_ref(ref, g_ref, b_ref, rows):
    def body(r, carry):
        sl = pl.ds(pl.multiple_of(r * rows, rows), rows)
        ref[sl, :] = _layer_norm_rows(ref[sl, :], g_ref[...], b_ref[...])
        return carry
    lax.fori_loop(0, ref.shape[0] // rows, body, 0)


def _outproj_kernel(alpha, tn, oa_ref, ob_ref, wa_ref, wb_ref, x_ref, g_ref, b_ref, hid_ref, hidt_ref):
    j = pl.program_id(1)
    col = pl.multiple_of(j * tn, LANE)
    hid_ref[:, pl.ds(col, tn)] = (alpha * x_ref[...] + _dot(oa_ref[...], wa_ref[...])
                                  + _dot(ob_ref[...], wb_ref[...]))

    @pl.when(j == pl.num_programs(1) - 1)
    def _():
        tm, d = hid_ref.shape
        _layer_norm_ref(hid_ref, g_ref, b_ref, min(tm, 32))
        rb = min(tm, LANE)
        step = _tile(d, 512)
        for r in range(tm // rb):
            for c in range(d // step):
                hidt_ref[c * step:(c + 1) * step, r * rb:(r + 1) * rb] = (
                    hid_ref[r * rb:(r + 1) * rb, c * step:(c + 1) * step].T.astype(BF16))


def _outproj(o_a, o_b, w_a, w_b, x2d, g, b, alpha):
    m, d = x2d.shape
    tm = _tile(m, 512)
    tn = _tile(d, 512)
    aw, bw = o_a.shape[1], o_b.shape[1]
    assert w_a.shape[0] == aw + bw and aw % bw == 0
    return pl.pallas_call(
        functools.partial(_outproj_kernel, alpha, tn),
        grid=(m // tm, d // tn),
        in_specs=[
            pl.BlockSpec((tm, o_a.shape[1]), lambda i, j: (i, 0)),
            pl.BlockSpec((tm, o_b.shape[1]), lambda i, j: (i, 0)),
            pl.BlockSpec((aw, tn), lambda i, j: (0, j)),
            pl.BlockSpec((bw, tn), lambda i, j: (aw // bw, j)),
            pl.BlockSpec((tm, tn), lambda i, j: (i, j)),
            pl.BlockSpec((1, d), lambda i, j: (0, 0)),
            pl.BlockSpec((1, d), lambda i, j: (0, 0)),
        ],
        out_specs=[
            pl.BlockSpec((tm, d), lambda i, j: (i, 0)),
            pl.BlockSpec((d, tm), lambda i, j: (0, i)),
        ],
        out_shape=[
            jax.ShapeDtypeStruct((m, d), F32),
            jax.ShapeDtypeStruct((d, m), BF16),
        ],
        compiler_params=_params(("parallel", "arbitrary")),
        name="outproj",
    )(o_a, o_b, w_a, w_b, x2d, g, b)


def _top_values(x, n):
    vals = []
    for _ in range(n):
        m = jnp.max(x, axis=0, keepdims=True)
        vals.append(m)
        x = jnp.where(x == m, NEG_INF, x)
    return vals


def _route_kernel(topk, ht_ref, wq_ref, key_ref, a1_ref, a2_ref, thr_ref, cand_scr):
    dk = key_ref.shape[3]
    n = topk + 1
    qt = _dot(wq_ref[...], ht_ref[...])
    s1 = _dot(key_ref[0, 0], qt[:dk].astype(BF16))
    s2 = _dot(key_ref[0, 1], qt[dk:].astype(BF16))
    top1 = _top_values(s1, n)
    top2 = _top_values(s2, n)
    cand_scr[...] = jnp.full(cand_scr.shape, NEG_INF, F32)
    pairs = [(a, b) for a in range(n) for b in range(n) if (a + 1) * (b + 1) <= n]
    for r, (a, b) in enumerate(pairs):
        cand_scr[r:r + 1, :] = top1[a] + top2[b]
    best = _top_values(cand_scr[...], n)
    z = jnp.exp(best[0] - best[0])
    for t in best[1:topk]:
        z = z + jnp.exp(t - best[0])
    a1_ref[0] = jnp.exp(s1 - top1[0]) / z
    a2_ref[0] = jnp.exp(s2 - top2[0])
    thr = jnp.exp(0.5 * (best[topk - 1] + best[topk]) - best[0]) / z
    thr_ref[0] = jnp.broadcast_to(thr, thr_ref.shape[1:])


def _num_candidates(n):
    return -(-sum(n // (a + 1) for a in range(n)) // 8) * 8


def _route(hid_t, wq_t, keys, topk):
    d, m = hid_t.shape
    ph, _, nk, dk = keys.shape
    tm = _tile(m, 512)
    out = jax.ShapeDtypeStruct((ph, nk, m), F32)
    blk = pl.BlockSpec((1, nk, tm), lambda i, h: (h, 0, i))
    return pl.pallas_call(
        functools.partial(_route_kernel, topk),
        grid=(m // tm, ph),
        in_specs=[
            pl.BlockSpec((d, tm), lambda i, h: (0, i)),
            pl.BlockSpec((2 * dk, d), lambda i, h: (h, 0)),
            pl.BlockSpec((1, 2, nk, dk), lambda i, h: (h, 0, 0, 0)),
        ],
        out_specs=[blk, blk, pl.BlockSpec((1, 8, tm), lambda i, h: (h, 0, i))],
        out_shape=[out, out, jax.ShapeDtypeStruct((ph, 8, m), F32)],
        scratch_shapes=[pltpu.VMEM((_num_candidates(topk + 1), tm), F32)],
        compiler_params=_params(("parallel", "arbitrary")),
        name="peer_route",
    )(hid_t, wq_t, keys)


def _peer_kernel(ht_ref, u_ref, vt_ref, a1_ref, a2_ref, thr_ref, o_ref, act_scr, pre_scr):
    e = pl.program_id(1)
    n_tiles = pl.num_programs(1) - 1
    ph, nk, tm = a2_ref.shape
    te = u_ref.shape[0]
    n_sub = te // nk
    d = o_ref.shape[0]
    step = d // n_sub
    slot = e % 2
    strip = min(tm, LANE)

    @pl.when(e == 0)
    def _():
        o_ref[...] = jnp.zeros(o_ref.shape, F32)
        act_scr[1] = jnp.zeros(act_scr.shape[1:], BF16)

    @pl.when(e < n_tiles)
    def _():
        pre_scr[...] = _dot(u_ref[...], ht_ref[...])

    def body(c, carry):
        sl = pl.ds(pl.multiple_of(c * step, step), step)
        o_ref[sl, :] += _dot(vt_ref[sl, :], act_scr[1 - slot])
        row = jnp.minimum(e * n_sub + c, nk - 1)
        rs = pl.ds(pl.multiple_of(c * nk, nk), nk)
        a1_rows = [a1_ref[h, pl.ds(row, 1), :] for h in range(ph)]
        for t in range(tm // strip):
            ls = slice(t * strip, (t + 1) * strip)
            gate = None
            for h in range(ph):
                g = a2_ref[h, :, ls] * a1_rows[h][:, ls]
                term = jnp.where(g >= thr_ref[h, 0:1, ls], g, 0.0)
                gate = term if gate is None else gate + term
            x = pre_scr[rs, ls]
            act = gate * (0.5 * x * (1.0 + lax.erf(x * (2.0 ** -0.5))))
            act_scr[slot, rs, ls] = act.astype(BF16)
        return carry
    lax.fori_loop(0, n_sub, body, 0, unroll=True)


def _peer(hid_t, u_b, v_t, a1, a2, thr):
    d, m = hid_t.shape
    ne = u_b.shape[0]
    ph, nk, _ = a1.shape
    tm = _tile(m, 512)
    te = _tile(ne, 512)
    n_tiles = ne // te
    assert te % nk == 0 and d % (te // nk) == 0
    rt = pl.BlockSpec((ph, nk, tm), lambda i, e: (0, 0, i))
    return pl.pallas_call(
        _peer_kernel,
        grid=(m // tm, n_tiles + 1),
        in_specs=[
            pl.BlockSpec((d, tm), lambda i, e: (0, i)),
            pl.BlockSpec((te, d), lambda i, e: (jnp.minimum(e, n_tiles - 1), 0)),
            pl.BlockSpec((d, te), lambda i, e: (0, jnp.maximum(e - 1, 0))),
            rt, rt,
            pl.BlockSpec((ph, 8, tm), lambda i, e: (0, 0, i)),
        ],
        out_specs=pl.BlockSpec((d, tm), lambda i, e: (0, i)),
        out_shape=jax.ShapeDtypeStruct((d, m), F32),
        scratch_shapes=[pltpu.VMEM((2, te, tm), BF16), pltpu.VMEM((te, tm), F32)],
        compiler_params=_params(("parallel", "arbitrary")),
        name="peer_dense",
    )(hid_t, u_b, v_t, a1, a2, thr)


def _final_kernel(alpha, hid_ref, pt_ref, g_ref, b_ref, y_ref):
    d = hid_ref.shape[1]
    step = _tile(d, 512)
    for c in range(d // step):
        sl = slice(c * step, (c + 1) * step)
        y_ref[:, sl] = alpha * hid_ref[:, sl] + pt_ref[sl, :].T
    _layer_norm_ref(y_ref, g_ref, b_ref, min(y_ref.shape[0], 32))


def _final(hid, peer_t, g, b, alpha):
    m, d = hid.shape
    tm = _tile(m, 256)
    return pl.pallas_call(
        functools.partial(_final_kernel, alpha),
        grid=(m // tm,),
        in_specs=[
            pl.BlockSpec((tm, d), lambda i: (i, 0)),
            pl.BlockSpec((d, tm), lambda i: (0, i)),
            pl.BlockSpec((1, d), lambda i: (0, 0)),
            pl.BlockSpec((1, d), lambda i: (0, 0)),
        ],
        out_specs=pl.BlockSpec((tm, d), lambda i: (i, 0)),
        out_shape=jax.ShapeDtypeStruct((m, d), F32),
        compiler_params=_params(("parallel",)),
        name="final_ln",
    )(hid, peer_t, g, b)


def _pad_lanes(a):
    return jnp.pad(a, [(0, 0)] * (a.ndim - 1) + [(0, LANE - a.shape[-1])])


def _trunk(x, conv_hist, s0, fox_cache, wts, depth):
    (w_main, w_small, conv_w, prow, pcol, norm_w, w_a, w_b, ln1_g, ln1_b, wq_t, keys, u_b, v_t,
     ln2_g, ln2_b, ah, bh, hd, topk) = wts
    b, t_len, d = x.shape
    m = b * t_len
    aw, bw = ah * hd, bh * hd
    alpha = (2 * depth) ** 0.25
    f_lane = 2 * ah
    x2d = x.reshape(m, d)

    n_a = 4 * aw
    prompt = fox_cache is None
    proj, k_new, v_new, sm, smt, *qv_t = _inproj(x2d, w_main, w_small, n_a, bw, prompt)
    proj3 = proj.reshape(b, t_len, -1)
    k_new = k_new.reshape(b, t_len, bw)
    v_new = v_new.reshape(b, t_len, bw)

    c_len = min(64, t_len)
    zero_c = jnp.zeros((b, 1, LANE), F32)
    zero_r = jnp.zeros((b, LANE, 1), F32)
    if fox_cache is None:
        carry_c, carry_r = zero_c, zero_r
    else:
        clf = fox_cache[2].astype(F32)
        p_len = clf.shape[1]
        clf_col = jnp.pad(clf, ((0, 0), (0, 0), (f_lane, LANE - f_lane - bh)))
        clf_row = jnp.swapaxes(clf_col, 1, 2)
        cc_col, _, cc_row = _gates(clf_col, clf_row, zero_c, zero_r, prow, pcol,
                                   apply=False, c_len=min(64, p_len), ah=ah, bh=bh)
        carry_c = cc_col[:, p_len - 1:, :]
        carry_r = cc_row[:, :, p_len - 1:]
    smt3 = jnp.swapaxes(smt.reshape(LANE, b, t_len), 0, 1)
    col, rowc, rowf = _gates(sm.reshape(b, t_len, LANE), smt3, carry_c, carry_r, prow, pcol,
                             apply=True, c_len=c_len, ah=ah, bh=bh)
    logf = col[:, :, f_lane + bh:f_lane + 2 * bh]

    o_a, s_new = _gdn(proj3, conv_hist, conv_w, col, rowc, s0, norm_w, c_len=c_len, ah=ah, hd=hd)
    conv_new = jnp.concatenate([conv_hist.astype(F32), proj3[:, :, :3 * aw]], axis=1)[:, -3:]

    qb0 = n_a // hd
    if prompt:
        o_b = _fox_t(k_new, qv_t[0], rowf, col, bh=bh, hd=hd, f_lane=f_lane)
    else:
        assert n_a % bw == 0
        k_cache = fox_cache[0].reshape(b, -1, hd).astype(F32)
        v_cache = fox_cache[1].reshape(b, -1, hd).astype(F32)
        o_b = _fox_s(proj3, n_a // bw, k_new, v_new, k_cache, v_cache, col, cc_row, rowf,
                     bh=bh, hd=hd, f_lane=f_lane)

    hid, hid_t = _outproj(o_a.reshape(m, aw), o_b.reshape(m, bw), w_a, w_b, x2d, ln1_g, ln1_b, alpha)
    a1, a2, thr = _route(hid_t, wq_t, keys, topk)
    peer_t = _peer(hid_t, u_b, v_t, a1, a2, thr)
    y = _final(hid, peer_t, ln2_g, ln2_b, alpha).reshape(b, t_len, d)
    return y, (k_new.reshape(b, t_len, bh, hd), v_new.reshape(b, t_len, bh, hd), logf, s_new, conv_new)


def kernel(x_prompt, x_sample, cache_fox_k, cache_fox_v, cache_fox_logf, state_gdn, state_gdn_conv,
           w_in, gdn_conv_w, gdn_a_log, gdn_dt_bias, gdn_norm_w, fox_f_bias, w_out, ln1_g, ln1_b,
           peer_w_q, peer_sub_keys, peer_u, peer_v, ln2_g, ln2_b):
    depth = w_in.shape[0]
    ah = gdn_a_log.shape[1]
    bh = fox_f_bias.shape[1]
    hd = gdn_norm_w.shape[1]
    aw, bw = ah * hd, bh * hd
    topk = 16
    assert 2 * ah + 2 * bh <= LANE
    n_p = x_prompt.shape[0]
    yp, ys = x_prompt, x_sample
    outs_p, outs_s = [], []
    for l in range(depth):
        o_a_a = 4 * aw
        o_b_qkv = o_a_a + 2 * ah
        o_b_f = o_b_qkv + 3 * bw
        wl = w_in[l]
        w_main = jnp.concatenate([wl[:, :o_a_a].astype(BF16), wl[:, o_b_qkv:o_b_f].astype(BF16)], axis=1)
        w_out_b = w_out[l].astype(BF16)
        w_f = wl[:, o_b_f:o_b_f + bh]
        w_small = _pad_lanes(jnp.concatenate([wl[:, o_a_a:o_b_qkv], w_f, w_f], axis=1)).astype(BF16)
        zeros_a = jnp.zeros((ah,), F32)
        prow = jnp.stack([
            _pad_lanes(gdn_a_log[l].astype(F32)),
            _pad_lanes(jnp.concatenate([gdn_dt_bias[l].astype(F32), zeros_a,
                                        fox_f_bias[l].astype(F32), fox_f_bias[l].astype(F32)])),
        ])
        wts = (w_main, w_small, gdn_conv_w[l].astype(F32), prow, prow.T,
               gdn_norm_w[l].reshape(1, hd).astype(F32),
               w_out_b, w_out_b,
               ln1_g[l].reshape(1, -1), ln1_b[l].reshape(1, -1),
               peer_w_q[l].T.astype(BF16), peer_sub_keys[l].astype(BF16),
               peer_u[l].astype(BF16), peer_v[l].T.astype(BF16),
               ln2_g[l].reshape(1, -1), ln2_b[l].reshape(1, -1), ah, bh, hd, topk)
        conv0 = jnp.zeros((n_p, 3, 3 * aw), yp.dtype)
        s0 = jnp.zeros((n_p, ah, hd, hd), yp.dtype)
        yp, st_p = _trunk(yp, conv0, s0, None, wts, depth)
        ys, st_s = _trunk(ys, state_gdn_conv[l], state_gdn[l],
                          (cache_fox_k[l], cache_fox_v[l], cache_fox_logf[l]), wts, depth)
        outs_p.append(st_p)
        outs_s.append(st_s)
    stack = lambda outs, n: jnp.stack([o[n] for o in outs], axis=0)
    return ((yp, ys) + tuple(stack(outs_p, n) for n in range(5))
            + tuple(stack(outs_s, n) for n in range(5)))
```

```python
import functools

import jax
import jax.numpy as jnp
from jax import lax
from jax.experimental import pallas as pl
from jax.experimental.pallas import tpu as pltpu

F32 = jnp.float32
BF16 = jnp.bfloat16
LANE = 128
LN_EPS = 1e-5
RMS_EPS = 1e-6
L2_EPS = 1e-6
CONV_PAD = 8
VMEM_LIMIT = 58 * 1024 * 1024
HIGHEST = lax.Precision.HIGHEST
NEG_INF = float("-inf")
LOG2E = 1.4426950408889634
PEER_DRAIN_TRIPS = 16


def _params(sem):
    return pltpu.CompilerParams(dimension_semantics=sem, vmem_limit_bytes=VMEM_LIMIT)


def _tile(n, pref, align=LANE):
    if n <= pref:
        return n
    t = (pref // align) * align
    while t >= align:
        if n % t == 0:
            return t
        t -= align
    return n


def _dot(a, b):
    return jnp.dot(a, b, preferred_element_type=F32)


def _dot_nt(a, b):
    return lax.dot_general(a, b, (((1,), (1,)), ((), ())), preferred_element_type=F32)


def _dot_tn(a, b):
    return lax.dot_general(a, b, (((0,), (0,)), ((), ())), preferred_element_type=F32)


def _sigmoid(x):
    return 1.0 / (1.0 + jnp.exp(-x))


def _softplus(x):
    return jnp.maximum(x, 0.0) + jnp.log1p(jnp.exp(-jnp.abs(x)))


def _silu(x):
    return x * _sigmoid(x)


def _inproj_kernel(j_q, j_k, j_v, emit_t, x_ref, w_ref, ws_ref, o_ref, k_ref, v_ref, s_ref, st_ref, *rest):
    xb_ref = rest[-1]
    j = pl.program_id(1)

    @pl.when(j == 0)
    def _():
        xb = x_ref[...].astype(BF16)
        xb_ref[...] = xb
        sm = _dot(xb, ws_ref[...])
        s_ref[...] = sm
        st_ref[...] = sm.T

    @pl.when(j < j_q)
    def _():
        o_ref[...] = _dot(xb_ref[...], w_ref[...])

    @pl.when((j >= j_q) & (j < j_k))
    def _():
        acc = _dot(xb_ref[...], w_ref[...])
        o_ref[...] = acc
        if emit_t:
            rest[0][...] = acc.T.astype(BF16)

    @pl.when((j >= j_k) & (j < j_v))
    def _():
        k_ref[...] = _dot(xb_ref[...], w_ref[...])

    @pl.when(j >= j_v)
    def _():
        acc = _dot(xb_ref[...], w_ref[...])
        v_ref[...] = acc
        if emit_t:
            rest[0][...] = acc.T.astype(BF16)


def _inproj(x2d, w_main, w_small, n_a, bw, emit_t):
    m, d = x2d.shape
    n = w_main.shape[1]
    assert n == n_a + 3 * bw
    tm = _tile(m, 512)
    tn = _tile(bw, 1024)
    assert n_a % tn == 0 and bw % tn == 0
    nb = bw // tn
    j_q = n_a // tn
    j_k = j_q + nb
    j_v = j_k + nb
    out_specs = [
        pl.BlockSpec((tm, tn), lambda i, j: (i, jnp.minimum(j, j_k - 1))),
        pl.BlockSpec((tm, tn), lambda i, j: (i, jnp.clip(j - j_k, 0, nb - 1))),
        pl.BlockSpec((tm, tn), lambda i, j: (i, jnp.maximum(j - j_v, 0))),
        pl.BlockSpec((tm, LANE), lambda i, j: (i, 0)),
        pl.BlockSpec((LANE, tm), lambda i, j: (0, i)),
    ]
    out_shape = [
        jax.ShapeDtypeStruct((m, n_a + bw), F32),
        jax.ShapeDtypeStruct((m, bw), F32),
        jax.ShapeDtypeStruct((m, bw), F32),
        jax.ShapeDtypeStruct((m, LANE), F32),
        jax.ShapeDtypeStruct((LANE, m), F32),
    ]
    if emit_t:
        row_blk = lambda j: jnp.where(j < j_k, jnp.maximum(j - j_q, 0),
                                      jnp.where(j < j_v, nb - 1, j - j_v + nb))
        out_specs.append(pl.BlockSpec((tn, tm), lambda i, j: (row_blk(j), i)))
        out_shape.append(jax.ShapeDtypeStruct((2 * bw, m), BF16))
    return pl.pallas_call(
        functools.partial(_inproj_kernel, j_q, j_k, j_v, emit_t),
        grid=(m // tm, n // tn),
        in_specs=[
            pl.BlockSpec((tm, d), lambda i, j: (i, 0)),
            pl.BlockSpec((d, tn), lambda i, j: (0, j)),
            pl.BlockSpec((d, LANE), lambda i, j: (0, 0)),
        ],
        out_specs=out_specs,
        out_shape=out_shape,
        scratch_shapes=[pltpu.VMEM((tm, d), BF16)],
        compiler_params=_params(("parallel", "arbitrary")),
        name="inproj",
    )(x2d, w_main, w_small)


def _gate_values(z, a_log, bias, idx, ah, bh):
    zz = z + bias
    g = -jnp.exp(a_log) * _softplus(zz)
    beta = _sigmoid(zz)
    logf = -_softplus(-zz)
    return jnp.where(idx < ah, g, jnp.where(idx < 2 * ah, beta, jnp.where(idx < 2 * ah + 2 * bh, logf, 0.0)))


def _gate_merge(idx, cs, y, carry, ah, bh):
    return jnp.where(idx < ah, cs,
                     jnp.where(idx < 2 * ah, y,
                               jnp.where(idx < 2 * ah + bh, cs + carry,
                                         jnp.where(idx < 2 * ah + 2 * bh, y, 0.0))))


def _gates_kernel(apply, c_len, ah, bh, sm_ref, smt_ref, cc_ref, cr_ref, prow_ref, pcol_ref,
                  col_ref, rowc_ref, rowf_ref):
    t_len = sm_ref.shape[1]
    nc = t_len // c_len
    ii = lax.broadcasted_iota(jnp.int32, (c_len, c_len), 0)
    jj = lax.broadcasted_iota(jnp.int32, (c_len, c_len), 1)
    tril = (ii >= jj).astype(F32)
    triu = (ii <= jj).astype(F32)
    lane = lax.broadcasted_iota(jnp.int32, (c_len, LANE), 1)
    subl = lax.broadcasted_iota(jnp.int32, (LANE, c_len), 0)
    carry_c = cc_ref[0]
    carry_r = cr_ref[0]
    for c in range(nc):
        z = sm_ref[0, c * c_len:(c + 1) * c_len, :]
        y = _gate_values(z, prow_ref[0:1, :], prow_ref[1:2, :], lane, ah, bh) if apply else z
        cs = jnp.dot(tril, y, precision=HIGHEST, preferred_element_type=F32)
        col_ref[0, c * c_len:(c + 1) * c_len, :] = _gate_merge(lane, cs, y, carry_c, ah, bh)
        carry_c = carry_c + cs[c_len - 1:c_len, :]

        zt = smt_ref[0, :, c * c_len:(c + 1) * c_len]
        yt = _gate_values(zt, pcol_ref[:, 0:1], pcol_ref[:, 1:2], subl, ah, bh) if apply else zt
        cst = jnp.dot(yt, triu, precision=HIGHEST, preferred_element_type=F32)
        out_t = _gate_merge(subl, cst, yt, carry_r, ah, bh)
        rowc_ref[0, c] = out_t
        rowf_ref[0, :, c * c_len:(c + 1) * c_len] = out_t
        carry_r = carry_r + cst[:, c_len - 1:c_len]


def _gates(sm3, smt, carry_col, carry_row, prow, pcol, *, apply, c_len, ah, bh):
    b, t_len, _ = sm3.shape
    nc = t_len // c_len
    return pl.pallas_call(
        functools.partial(_gates_kernel, apply, c_len, ah, bh),
        grid=(b,),
        in_specs=[
            pl.BlockSpec((1, t_len, LANE), lambda i: (i, 0, 0)),
            pl.BlockSpec((1, LANE, t_len), lambda i: (i, 0, 0)),
            pl.BlockSpec((1, 1, LANE), lambda i: (i, 0, 0)),
            pl.BlockSpec((1, LANE, 1), lambda i: (i, 0, 0)),
            pl.BlockSpec((2, LANE), lambda i: (0, 0)),
            pl.BlockSpec((LANE, 2), lambda i: (0, 0)),
        ],
        out_specs=[
            pl.BlockSpec((1, t_len, LANE), lambda i: (i, 0, 0)),
            pl.BlockSpec((1, nc, LANE, c_len), lambda i: (i, 0, 0, 0)),
            pl.BlockSpec((1, LANE, t_len), lambda i: (i, 0, 0)),
        ],
        out_shape=[
            jax.ShapeDtypeStruct((b, t_len, LANE), F32),
            jax.ShapeDtypeStruct((b, nc, LANE, c_len), F32),
            jax.ShapeDtypeStruct((b, LANE, t_len), F32),
        ],
        compiler_params=_params(("parallel",)),
        name="gates",
    )(sm3, smt, carry_col, carry_row, prow, pcol)


def _unit_lower_inverses(lows, eye, ii, jj):
    c_len = lows[0].shape[0]
    same0 = (ii >> 1) == (jj >> 1)
    ts = [eye - jnp.where(same0, low, 0.0) for low in lows]
    k = 1
    while (2 << k) <= c_len:
        sel = ((ii >> (k + 1)) == (jj >> (k + 1))) & (((ii >> k) & 1) == 1) & (((jj >> k) & 1) == 0)
        tbs = [t.astype(BF16) for t in ts]
        mid = [_dot(tb, jnp.where(sel, low, 0.0).astype(BF16)).astype(BF16) for tb, low in zip(tbs, lows)]
        ts = [t - _dot(m, tb) for t, m, tb in zip(ts, mid, tbs)]
        k += 1
    return ts


def _gdn_kernel(c_len, hb, ah, q_ref, k_ref, v_ref, z_ref, hq_ref, hk_ref, hv_ref,
                cq_ref, ck_ref, cv_ref, col_ref, rowc_ref, s0_ref, nw_ref,
                o_ref, sn_ref, s_scr, buf_scr):
    c_idx = pl.program_id(2)
    hd = nw_ref.shape[1]
    lo = CONV_PAD - 3

    @pl.when(c_idx == 0)
    def _():
        s_scr[...] = s0_ref[0]
        buf_scr[0, lo:CONV_PAD, :] = hq_ref[0]
        buf_scr[1, lo:CONV_PAD, :] = hk_ref[0]
        buf_scr[2, lo:CONV_PAD, :] = hv_ref[0]

    conv = []
    for n, (x_ref, w_ref) in enumerate(((q_ref, cq_ref), (k_ref, ck_ref), (v_ref, cv_ref))):
        buf_scr[n, CONV_PAD:CONV_PAD + c_len, :] = x_ref[0]
        acc = w_ref[0:1, :] * buf_scr[n, lo:lo + c_len, :]
        for w in range(1, 4):
            acc = acc + w_ref[w:w + 1, :] * buf_scr[n, lo + w:lo + w + c_len, :]
        buf_scr[n, lo:CONV_PAD, :] = buf_scr[n, lo + c_len:CONV_PAD + c_len, :]
        conv.append(_silu(acc))
    qc, kc, vc = conv

    colblk = col_ref[0]
    ii = lax.broadcasted_iota(jnp.int32, (c_len, c_len), 0)
    jj = lax.broadcasted_iota(jnp.int32, (c_len, c_len), 1)
    eye = (ii == jj).astype(F32)
    nw = nw_ref[...]
    heads = range(hb)
    sls = [slice(h * hd, (h + 1) * hd) for h in heads]

    cum_c = [colblk[:, h:h + 1] for h in heads]
    beta_c = [colblk[:, ah + h:ah + h + 1] for h in heads]
    cum_r = [rowc_ref[0, 0, h:h + 1, :] for h in heads]
    cum_last = [r[:, c_len - 1:c_len] for r in cum_r]
    e_cum = [jnp.exp(c) for c in cum_c]

    q = [qc[:, sl] for sl in sls]
    k = [kc[:, sl] for sl in sls]
    q = [x * (lax.rsqrt(jnp.sum(x * x, axis=1, keepdims=True) + L2_EPS) * (hd ** -0.5)) for x in q]
    k = [x * lax.rsqrt(jnp.sum(x * x, axis=1, keepdims=True) + L2_EPS) for x in k]
    qb = [x.astype(BF16) for x in q]
    kb = [x.astype(BF16) for x in k]

    decay = [jnp.exp(jnp.where(ii >= jj, c - r, NEG_INF)) for c, r in zip(cum_c, cum_r)]
    kk = [_dot_nt(x, x) for x in kb]
    qk = [_dot_nt(x, y) for x, y in zip(qb, kb)]
    lows = [jnp.where(ii > jj, b * m * d, 0.0) for b, m, d in zip(beta_c, kk, decay)]
    t_inv = _unit_lower_inverses(lows, eye, ii, jj)

    rhs = [jnp.concatenate([b * vc[:, sl], (b * e) * x], axis=1).astype(BF16)
           for b, e, x, sl in zip(beta_c, e_cum, k, sls)]
    w_all = [_dot(t.astype(BF16), r) for t, r in zip(t_inv, rhs)]

    s_old = [s_scr[h] for h in heads]
    sb = [s.astype(BF16) for s in s_old]
    u = [w[:, :hd] - _dot(w[:, hd:].astype(BF16), s) for w, s in zip(w_all, sb)]
    ub = [x.astype(BF16) for x in u]
    o = [_dot((x * e).astype(BF16), s) + _dot((m * d).astype(BF16), y)
         for x, e, s, m, d, y in zip(q, e_cum, sb, qk, decay, ub)]
    for h in heads:
        k_dec = k[h] * jnp.exp(cum_last[h] - cum_c[h])
        s_scr[h] = s_old[h] * jnp.exp(cum_last[h]) + _dot_tn(k_dec.astype(BF16), ub[h])
    for h in heads:
        x = o[h] * lax.rsqrt(jnp.mean(o[h] * o[h], axis=1, keepdims=True) + RMS_EPS) * nw
        o_ref[0, :, sls[h]] = (x * _silu(z_ref[0, :, sls[h]])).astype(o_ref.dtype)

    @pl.when(c_idx == pl.num_programs(2) - 1)
    def _():
        sn_ref[0] = s_scr[...]


def _gdn(proj3, conv_hist, conv_w, col, rowc, s0, norm_w, *, c_len, ah, hd):
    b, t_len, _ = proj3.shape
    nc = t_len // c_len
    aw = ah * hd
    hb = ah
    ng = ah // hb
    wb = hb * hd
    kq, kk_, kv, kz = 0, ng, 2 * ng, 3 * ng
    tok = lambda off: pl.BlockSpec((1, c_len, wb), lambda i, g, c: (i, c, off + g))
    hist = lambda off: pl.BlockSpec((1, 3, wb), lambda i, g, c: (i, 0, off + g))
    cw = lambda off: pl.BlockSpec((4, wb), lambda i, g, c: (0, off + g))
    return pl.pallas_call(
        functools.partial(_gdn_kernel, c_len, hb, ah),
        grid=(b, ng, nc),
        in_specs=[
            tok(kq), tok(kk_), tok(kv), tok(kz),
            hist(kq), hist(kk_), hist(kv),
            cw(kq), cw(kk_), cw(kv),
            pl.BlockSpec((1, c_len, LANE), lambda i, g, c: (i, c, 0)),
            pl.BlockSpec((1, 1, LANE, c_len), lambda i, g, c: (i, c, 0, 0)),
            pl.BlockSpec((1, hb, hd, hd), lambda i, g, c: (i, g, 0, 0)),
            pl.BlockSpec((1, hd), lambda i, g, c: (0, 0)),
        ],
        out_specs=[
            pl.BlockSpec((1, c_len, wb), lambda i, g, c: (i, c, g)),
            pl.BlockSpec((1, hb, hd, hd), lambda i, g, c: (i, g, 0, 0)),
        ],
        out_shape=[
            jax.ShapeDtypeStruct((b, t_len, aw), BF16),
            jax.ShapeDtypeStruct((b, ah, hd, hd), F32),
        ],
        scratch_shapes=[
            pltpu.VMEM((hb, hd, hd), F32),
            pltpu.VMEM((3, CONV_PAD + c_len, wb), F32),
        ],
        compiler_params=_params(("parallel", "parallel", "arbitrary")),
        name="gdn",
    )(proj3, proj3, proj3, proj3, conv_hist, conv_hist, conv_hist, conv_w, conv_w, conv_w,
      col, rowc, s0, norm_w)


def _fox_s_kernel(f_lane, bh, hd, q_ref, kn_ref, vn_ref, kc_ref, vc_ref, cq_ref, ckc_ref, ckn_ref, o_ref):
    t_q = q_ref.shape[1]
    p_len = kc_ref.shape[1] // bh
    heads = range(bh)
    sls = [slice(h * hd, (h + 1) * hd) for h in heads]
    q_scale = (hd ** -0.5) * LOG2E
    cqb = cq_ref[0] * LOG2E
    keep = (lax.broadcasted_iota(jnp.int32, (t_q, t_q), 1) <= lax.broadcasted_iota(jnp.int32, (t_q, t_q), 0))
    cq = [cqb[:, f_lane + h:f_lane + h + 1] for h in heads]
    ckc = [ckc_ref[0, f_lane + h:f_lane + h + 1, :] * LOG2E for h in heads]
    ckn = [ckn_ref[0, f_lane + h:f_lane + h + 1, :] * LOG2E for h in heads]
    qb = [(q_ref[0, :, sl] * q_scale).astype(BF16) for sl in sls]
    kc = [kc_ref[0, pl.ds(h, p_len, stride=bh), :].astype(BF16) for h in heads]
    xc = [_dot_nt(a, b) - c for a, b, c in zip(qb, kc, ckc)]
    xn = [jnp.where(keep, _dot_nt(a, kn_ref[0, :, sl].astype(BF16)) - c, NEG_INF)
          for a, sl, c in zip(qb, sls, ckn)]
    m = [c + jnp.maximum(jnp.max(a, axis=1, keepdims=True), jnp.max(b, axis=1, keepdims=True))
         for a, b, c in zip(xc, xn, cq)]
    pc = [jnp.exp2(a + (c - mm)) for a, c, mm in zip(xc, cq, m)]
    pn = [jnp.exp2(a + (c - mm)) for a, c, mm in zip(xn, cq, m)]
    vc = [vc_ref[0, pl.ds(h, p_len, stride=bh), :].astype(BF16) for h in heads]
    acc = [_dot(a.astype(BF16), b) + _dot(c.astype(BF16), vn_ref[0, :, sl].astype(BF16))
           for a, b, c, sl in zip(pc, vc, pn, sls)]
    for h in heads:
        den = jnp.sum(pc[h], axis=1, keepdims=True) + jnp.sum(pn[h], axis=1, keepdims=True)
        o_ref[0, :, sls[h]] = (acc[h] / den).astype(o_ref.dtype)


def _fox_s(q_arr, q_blk, k_new, v_new, k_cache, v_cache, cq_col, ck_cache_row, ck_new_row, *, bh, hd, f_lane):
    b, t_q, bw = k_new.shape
    rows = k_cache.shape[1]
    p_len = rows // bh
    return pl.pallas_call(
        functools.partial(_fox_s_kernel, f_lane, bh, hd),
        grid=(b,),
        in_specs=[
            pl.BlockSpec((1, t_q, bw), lambda i: (i, 0, q_blk)),
            pl.BlockSpec((1, t_q, bw), lambda i: (i, 0, 0)),
            pl.BlockSpec((1, t_q, bw), lambda i: (i, 0, 0)),
            pl.BlockSpec((1, rows, hd), lambda i: (i, 0, 0)),
            pl.BlockSpec((1, rows, hd), lambda i: (i, 0, 0)),
            pl.BlockSpec((1, t_q, LANE), lambda i: (i, 0, 0)),
            pl.BlockSpec((1, LANE, p_len), lambda i: (i, 0, 0)),
            pl.BlockSpec((1, LANE, t_q), lambda i: (i, 0, 0)),
        ],
        out_specs=pl.BlockSpec((1, t_q, bw), lambda i: (i, 0, 0)),
        out_shape=jax.ShapeDtypeStruct((b, t_q, bw), BF16),
        compiler_params=_params(("parallel",)),
        name="fox_s",
    )(q_arr, k_new, v_new, k_cache, v_cache, cq_col, ck_cache_row, ck_new_row)


def _fox_t_kernel(f_lane, hg, hd, k_ref, qt_ref, vt_ref, cq_ref, ck_ref, o_ref, m_scr, l_scr, acc_scr):
    g = pl.program_id(1)
    qi = pl.program_id(2)
    kj = pl.program_id(3)
    tk = k_ref.shape[1]
    tq = qt_ref.shape[1]
    heads = range(hg)
    sls = [slice(h * hd, (h + 1) * hd) for h in heads]
    k_scale = (hd ** -0.5) * LOG2E

    @pl.when(kj == 0)
    def _():
        m_scr[...] = jnp.full(m_scr.shape, NEG_INF, F32)
        l_scr[...] = jnp.zeros(l_scr.shape, F32)
        acc_scr[...] = jnp.zeros(acc_scr.shape, F32)

    def update(masked):
        ckb = ck_ref[0] * LOG2E
        lane = lax.broadcasted_iota(jnp.int32, ckb.shape, 1)
        ck = [jnp.sum(jnp.where(lane == f_lane + g * hg + h, ckb, 0.0), axis=1, keepdims=True)
              for h in heads]
        cq = [cq_ref[0, pl.ds(f_lane + g * hg + h, 1), :] * LOG2E for h in heads]
        kb = [(k_ref[0, :, sl] * k_scale).astype(BF16) for sl in sls]
        x = [_dot(a, qt_ref[sl, :]) - c for a, sl, c in zip(kb, sls, ck)]
        if masked:
            kpos = kj * tk + lax.broadcasted_iota(jnp.int32, (tk, tq), 0)
            qpos = qi * tq + lax.broadcasted_iota(jnp.int32, (tk, tq), 1)
            keep = kpos <= qpos
            x = [jnp.where(keep, a, NEG_INF) for a in x]
        m_old = [m_scr[h] for h in heads]
        m_new = [jnp.maximum(mo, c + jnp.max(a, axis=0, keepdims=True)) for mo, c, a in zip(m_old, cq, x)]
        p = [jnp.exp2(a + (c - mn)) for a, c, mn in zip(x, cq, m_new)]
        alpha = [jnp.exp2(mo - mn) for mo, mn in zip(m_old, m_new)]
        pv = [_dot(vt_ref[sl, :], a.astype(BF16)) for a, sl in zip(p, sls)]
        for h in heads:
            l_scr[h] = alpha[h] * l_scr[h] + jnp.sum(p[h], axis=0, keepdims=True)
            acc_scr[h] = alpha[h] * acc_scr[h] + pv[h]
            m_scr[h] = m_new[h]

    active = kj * tk <= qi * tq + (tq - 1)
    crosses = kj * tk + (tk - 1) > qi * tq

    @pl.when(active & crosses)
    def _():
        update(True)

    @pl.when(active & jnp.logical_not(crosses))
    def _():
        update(False)

    @pl.when(kj == pl.num_programs(3) - 1)
    def _():
        for h in heads:
            o_ref[0, :, sls[h]] = (acc_scr[h] / l_scr[h]).T.astype(o_ref.dtype)


def _fox_t(k3, qv_t, cq_row, ck_col, *, bh, hd, f_lane):
    b, t_len, bw = k3.shape
    tq = _tile(t_len, 512)
    tk = tq
    nq = t_len // tq
    hg = 4 if bh % 4 == 0 else bh
    wg = hg * hd
    ng = bh // hg
    last = lambda qi: (qi * tq + (tq - 1)) // tk
    kblk = lambda qi, kj: jnp.minimum(kj, last(qi))
    return pl.pallas_call(
        functools.partial(_fox_t_kernel, f_lane, hg, hd),
        grid=(b, ng, nq, nq),
        in_specs=[
            pl.BlockSpec((1, tk, wg), lambda i, g, qi, kj: (i, kblk(qi, kj), g)),
            pl.BlockSpec((wg, tq), lambda i, g, qi, kj: (g, i * nq + qi)),
            pl.BlockSpec((wg, tk), lambda i, g, qi, kj: (ng + g, i * nq + kblk(qi, kj))),
            pl.BlockSpec((1, LANE, tq), lambda i, g, qi, kj: (i, 0, qi)),
            pl.BlockSpec((1, tk, LANE), lambda i, g, qi, kj: (i, kblk(qi, kj), 0)),
        ],
        out_specs=pl.BlockSpec((1, tq, wg), lambda i, g, qi, kj: (i, qi, g)),
        out_shape=jax.ShapeDtypeStruct((b, t_len, bw), BF16),
        scratch_shapes=[
            pltpu.VMEM((hg, 1, tq), F32),
            pltpu.VMEM((hg, 1, tq), F32),
            pltpu.VMEM((hg, hd, tq), F32),
        ],
        compiler_params=_params(("parallel", "parallel", "parallel", "arbitrary")),
        name="fox_t",
    )(k3, qv_t, qv_t, cq_row, ck_col)


def _layer_norm_rows(x, g, b):
    mu = jnp.mean(x, axis=1, keepdims=True)
    xc = x - mu
    var = jnp.mean(xc * xc, axis=1, keepdims=True)
    return xc * lax.rsqrt(var + LN_EPS) * g + b


def _layer_norm_ref(ref, g_ref, b_ref, rows):
    def body(r, carry):
        sl = pl.ds(pl.multiple_of(r * rows, rows), rows)
        ref[sl, :] = _layer_norm_rows(ref[sl, :], g_ref[...], b_ref[...])
        return carry
    lax.fori_loop(0, ref.shape[0] // rows, body, 0)


def _outproj_kernel(alpha, tn, oa_ref, ob_ref, wa_ref, wb_ref, x_ref, g_ref, b_ref, hid_ref, hidt_ref):
    j = pl.program_id(1)
    col = pl.multiple_of(j * tn, LANE)
    hid_ref[:, pl.ds(col, tn)] = (alpha * x_ref[...] + _dot(oa_ref[...], wa_ref[...])
                                  + _dot(ob_ref[...], wb_ref[...]))

    @pl.when(j == pl.num_programs(1) - 1)
    def _():
        tm, d = hid_ref.shape
        _layer_norm_ref(hid_ref, g_ref, b_ref, min(tm, 32))
        rb = min(tm, LANE)
        step = _tile(d, 512)
        for r in range(tm // rb):
            for c in range(d // step):
                hidt_ref[c * step:(c + 1) * step, r * rb:(r + 1) * rb] = (
                    hid_ref[r * rb:(r + 1) * rb, c * step:(c + 1) * step].T.astype(BF16))


def _outproj(o_a, o_b, w_a, w_b, x2d, g, b, alpha):
    m, d = x2d.shape
    tm = _tile(m, 512)
    tn = _tile(d, 512)
    aw, bw = o_a.shape[1], o_b.shape[1]
    assert w_a.shape[0] == aw + bw and aw % bw == 0
    return pl.pallas_call(
        functools.partial(_outproj_kernel, alpha, tn),
        grid=(m // tm, d // tn),
        in_specs=[
            pl.BlockSpec((tm, o_a.shape[1]), lambda i, j: (i, 0)),
            pl.BlockSpec((tm, o_b.shape[1]), lambda i, j: (i, 0)),
            pl.BlockSpec((aw, tn), lambda i, j: (0, j)),
            pl.BlockSpec((bw, tn), lambda i, j: (aw // bw, j)),
            pl.BlockSpec((tm, tn), lambda i, j: (i, j)),
            pl.BlockSpec((1, d), lambda i, j: (0, 0)),
            pl.BlockSpec((1, d), lambda i, j: (0, 0)),
        ],
        out_specs=[
            pl.BlockSpec((tm, d), lambda i, j: (i, 0)),
            pl.BlockSpec((d, tm), lambda i, j: (0, i)),
        ],
        out_shape=[
            jax.ShapeDtypeStruct((m, d), F32),
            jax.ShapeDtypeStruct((d, m), BF16),
        ],
        compiler_params=_params(("parallel", "arbitrary")),
        name="outproj",
    )(o_a, o_b, w_a, w_b, x2d, g, b)


def _top_values(x, n):
    vals = []
    for _ in range(n):
        m = jnp.max(x, axis=0, keepdims=True)
        vals.append(m)
        x = jnp.where(x == m, NEG_INF, x)
    return vals


def _route_kernel(topk, hp, ht_ref, wq_ref, key_ref, a1_ref, a2_ref, thr_ref, cand_scr):
    dk = key_ref.shape[3]
    n = topk + 1
    pairs = [(a, b) for a in range(n) for b in range(n) if (a + 1) * (b + 1) <= n]

    def scores(h):
        qt = _dot(wq_ref[h * 2 * dk:(h + 1) * 2 * dk, :], ht_ref[...])
        return (_dot(key_ref[h, 0], qt[:dk].astype(BF16)),
                _dot(key_ref[h, 1], qt[dk:].astype(BF16)))

    nxt = scores(0)
    for h in range(hp):
        s1, s2 = nxt
        if h + 1 < hp:
            nxt = scores(h + 1)
        top1 = _top_values(s1, n)
        top2 = _top_values(s2, n)
        cand_scr[...] = jnp.full(cand_scr.shape, NEG_INF, F32)
        for r, (a, b) in enumerate(pairs):
            cand_scr[r:r + 1, :] = top1[a] + top2[b]
        best = _top_values(cand_scr[...], n)
        z = jnp.exp(best[0] - best[0])
        for t in best[1:topk]:
            z = z + jnp.exp(t - best[0])
        a1_ref[h] = jnp.exp(s1 - top1[0]) / z
        a2_ref[h] = jnp.exp(s2 - top2[0])
        thr = jnp.exp(0.5 * (best[topk - 1] + best[topk]) - best[0]) / z
        thr_ref[h] = jnp.broadcast_to(thr, thr_ref.shape[1:])


def _num_candidates(n):
    return -(-sum(n // (a + 1) for a in range(n)) // 8) * 8


def _route(hid_t, wq_t, keys, topk):
    d, m = hid_t.shape
    ph, _, nk, dk = keys.shape
    tm = _tile(m, 512)
    hp = 4 if ph % 4 == 0 else ph
    out = jax.ShapeDtypeStruct((ph, nk, m), F32)
    blk = pl.BlockSpec((hp, nk, tm), lambda i, h: (h, 0, i))
    return pl.pallas_call(
        functools.partial(_route_kernel, topk, hp),
        grid=(m // tm, ph // hp),
        in_specs=[
            pl.BlockSpec((d, tm), lambda i, h: (0, i)),
            pl.BlockSpec((hp * 2 * dk, d), lambda i, h: (h, 0)),
            pl.BlockSpec((hp, 2, nk, dk), lambda i, h: (h, 0, 0, 0)),
        ],
        out_specs=[blk, blk, pl.BlockSpec((hp, 8, tm), lambda i, h: (h, 0, i))],
        out_shape=[out, out, jax.ShapeDtypeStruct((ph, 8, m), F32)],
        scratch_shapes=[pltpu.VMEM((_num_candidates(topk + 1), tm), F32)],
        compiler_params=_params(("parallel", "arbitrary")),
        name="peer_route",
    )(hid_t, wq_t, keys)


def _peer_kernel(ht_ref, u_ref, vt_ref, a1_ref, a2_ref, thr_ref, o_ref, act_scr, pre_scr):
    e = pl.program_id(1)
    n_tiles = pl.num_programs(1) - 1
    ph, nk, tm = a2_ref.shape
    te = u_ref.shape[0]
    n_sub = te // nk
    d = o_ref.shape[0]
    slot = e % 2
    strip = min(tm, LANE)

    @pl.when(e == 0)
    def _():
        o_ref[...] = jnp.zeros(o_ref.shape, F32)
        act_scr[1] = jnp.zeros(act_scr.shape[1:], BF16)

    @pl.when(e < n_tiles)
    def _():
        pre_scr[...] = _dot(u_ref[...], ht_ref[...])

    n_ch = PEER_DRAIN_TRIPS
    step = d // n_ch
    gr = te // n_ch
    for c in range(n_ch):
        sl = slice(c * step, (c + 1) * step)
        o_ref[sl, :] += _dot(vt_ref[sl, :], act_scr[1 - slot])
        for r0 in range(c * gr, (c + 1) * gr, min(gr, nk)):
            rn = min(gr, nk)
            row = jnp.minimum(e * n_sub + r0 // nk, nk - 1)
            j0 = r0 % nk
            a1_rows = [a1_ref[h, pl.ds(row, 1), :] for h in range(ph)]
            for t in range(tm // strip):
                ls = slice(t * strip, (t + 1) * strip)
                gate = None
                for h in range(ph):
                    g = a2_ref[h, j0:j0 + rn, ls] * a1_rows[h][:, ls]
                    term = jnp.where(g >= thr_ref[h, 0:1, ls], g, 0.0)
                    gate = term if gate is None else gate + term
                x = pre_scr[r0:r0 + rn, ls]
                act = gate * (0.5 * x * (1.0 + lax.erf(x * (2.0 ** -0.5))))
                act_scr[slot, r0:r0 + rn, ls] = act.astype(BF16)


def _peer(hid_t, u_b, v_t, a1, a2, thr):
    d, m = hid_t.shape
    ne = u_b.shape[0]
    ph, nk, _ = a1.shape
    tm = _tile(m, 512)
    te = _tile(ne, 512)
    n_tiles = ne // te
    assert te % nk == 0 and d % (te // nk) == 0
    rt = pl.BlockSpec((ph, nk, tm), lambda i, e: (0, 0, i))
    return pl.pallas_call(
        _peer_kernel,
        grid=(m // tm, n_tiles + 1),
        in_specs=[
            pl.BlockSpec((d, tm), lambda i, e: (0, i)),
            pl.BlockSpec((te, d), lambda i, e: (jnp.minimum(e, n_tiles - 1), 0)),
            pl.BlockSpec((d, te), lambda i, e: (0, jnp.maximum(e - 1, 0))),
            rt, rt,
            pl.BlockSpec((ph, 8, tm), lambda i, e: (0, 0, i)),
        ],
        out_specs=pl.BlockSpec((d, tm), lambda i, e: (0, i)),
        out_shape=jax.ShapeDtypeStruct((d, m), F32),
        scratch_shapes=[pltpu.VMEM((2, te, tm), BF16), pltpu.VMEM((te, tm), F32)],
        compiler_params=_params(("parallel", "arbitrary")),
        name="peer_dense",
    )(hid_t, u_b, v_t, a1, a2, thr)


def _final_kernel(alpha, hid_ref, pt_ref, g_ref, b_ref, y_ref):
    d = hid_ref.shape[1]
    step = _tile(d, 512)
    for c in range(d // step):
        sl = slice(c * step, (c + 1) * step)
        y_ref[:, sl] = alpha * hid_ref[:, sl] + pt_ref[sl, :].T
    _layer_norm_ref(y_ref, g_ref, b_ref, min(y_ref.shape[0], 32))


def _final(hid, peer_t, g, b, alpha):
    m, d = hid.shape
    tm = _tile(m, 256)
    return pl.pallas_call(
        functools.partial(_final_kernel, alpha),
        grid=(m // tm,),
        in_specs=[
            pl.BlockSpec((tm, d), lambda i: (i, 0)),
            pl.BlockSpec((d, tm), lambda i: (0, i)),
            pl.BlockSpec((1, d), lambda i: (0, 0)),
            pl.BlockSpec((1, d), lambda i: (0, 0)),
        ],
        out_specs=pl.BlockSpec((tm, d), lambda i: (i, 0)),
        out_shape=jax.ShapeDtypeStruct((m, d), F32),
        compiler_params=_params(("parallel",)),
        name="final_ln",
    )(hid, peer_t, g, b)


def _pad_lanes(a):
    return jnp.pad(a, [(0, 0)] * (a.ndim - 1) + [(0, LANE - a.shape[-1])])


def _trunk(x, conv_hist, s0, fox_cache, wts, depth):
    (w_main, w_small, conv_w, prow, pcol, norm_w, w_a, w_b, ln1_g, ln1_b, wq_t, keys, u_b, v_t,
     ln2_g, ln2_b, ah, bh, hd, topk) = wts
    b, t_len, d = x.shape
    m = b * t_len
    aw, bw = ah * hd, bh * hd
    alpha = (2 * depth) ** 0.25
    f_lane = 2 * ah
    x2d = x.reshape(m, d)

    n_a = 4 * aw
    prompt = fox_cache is None
    proj, k_new, v_new, sm, smt, *qv_t = _inproj(x2d, w_main, w_small, n_a, bw, prompt)
    proj3 = proj.reshape(b, t_len, -1)
    k_new = k_new.reshape(b, t_len, bw)
    v_new = v_new.reshape(b, t_len, bw)

    c_len = min(64, t_len)
    zero_c = jnp.zeros((b, 1, LANE), F32)
    zero_r = jnp.zeros((b, LANE, 1), F32)
    if fox_cache is None:
        carry_c, carry_r = zero_c, zero_r
    else:
        clf = fox_cache[2].astype(F32)
        p_len = clf.shape[1]
        clf_col = jnp.pad(clf, ((0, 0), (0, 0), (f_lane, LANE - f_lane - bh)))
        clf_row = jnp.swapaxes(clf_col, 1, 2)
        cc_col, _, cc_row = _gates(clf_col, clf_row, zero_c, zero_r, prow, pcol,
                                   apply=False, c_len=min(64, p_len), ah=ah, bh=bh)
        carry_c = cc_col[:, p_len - 1:, :]
        carry_r = cc_row[:, :, p_len - 1:]
    smt3 = jnp.swapaxes(smt.reshape(LANE, b, t_len), 0, 1)
    col, rowc, rowf = _gates(sm.reshape(b, t_len, LANE), smt3, carry_c, carry_r, prow, pcol,
                             apply=True, c_len=c_len, ah=ah, bh=bh)
    logf = col[:, :, f_lane + bh:f_lane + 2 * bh]

    o_a, s_new = _gdn(proj3, conv_hist, conv_w, col, rowc, s0, norm_w, c_len=c_len, ah=ah, hd=hd)
    conv_new = jnp.concatenate([conv_hist.astype(F32), proj3[:, :, :3 * aw]], axis=1)[:, -3:]

    qb0 = n_a // hd
    if prompt:
        o_b = _fox_t(k_new, qv_t[0], rowf, col, bh=bh, hd=hd, f_lane=f_lane)
    else:
        assert n_a % bw == 0
        k_cache = fox_cache[0].reshape(b, -1, hd).astype(F32)
        v_cache = fox_cache[1].reshape(b, -1, hd).astype(F32)
        o_b = _fox_s(proj3, n_a // bw, k_new, v_new, k_cache, v_cache, col, cc_row, rowf,
                     bh=bh, hd=hd, f_lane=f_lane)

    hid, hid_t = _outproj(o_a.reshape(m, aw), o_b.reshape(m, bw), w_a, w_b, x2d, ln1_g, ln1_b, alpha)
    a1, a2, thr = _route(hid_t, wq_t, keys, topk)
    peer_t = _peer(hid_t, u_b, v_t, a1, a2, thr)
    y = _final(hid, peer_t, ln2_g, ln2_b, alpha).reshape(b, t_len, d)
    return y, (k_new.reshape(b, t_len, bh, hd), v_new.reshape(b, t_len, bh, hd), logf, s_new, conv_new)


def kernel(x_prompt, x_sample, cache_fox_k, cache_fox_v, cache_fox_logf, state_gdn, state_gdn_conv,
           w_in, gdn_conv_w, gdn_a_log, gdn_dt_bias, gdn_norm_w, fox_f_bias, w_out, ln1_g, ln1_b,
           peer_w_q, peer_sub_keys, peer_u, peer_v, ln2_g, ln2_b):
    depth = w_in.shape[0]
    ah = gdn_a_log.shape[1]
    bh = fox_f_bias.shape[1]
    hd = gdn_norm_w.shape[1]
    aw, bw = ah * hd, bh * hd
    topk = 16
    assert 2 * ah + 2 * bh <= LANE
    n_p = x_prompt.shape[0]
    yp, ys = x_prompt, x_sample
    outs_p, outs_s = [], []
    for l in range(depth):
        o_a_a = 4 * aw
        o_b_qkv = o_a_a + 2 * ah
        o_b_f = o_b_qkv + 3 * bw
        wl = w_in[l]
        w_main = jnp.concatenate([wl[:, :o_a_a].astype(BF16), wl[:, o_b_qkv:o_b_f].astype(BF16)], axis=1)
        w_out_b = w_out[l].astype(BF16)
        w_f = wl[:, o_b_f:o_b_f + bh]
        w_small = _pad_lanes(jnp.concatenate([wl[:, o_a_a:o_b_qkv], w_f, w_f], axis=1)).astype(BF16)
        zeros_a = jnp.zeros((ah,), F32)
        prow = jnp.stack([
            _pad_lanes(gdn_a_log[l].astype(F32)),
            _pad_lanes(jnp.concatenate([gdn_dt_bias[l].astype(F32), zeros_a,
                                        fox_f_bias[l].astype(F32), fox_f_bias[l].astype(F32)])),
        ])
        wts = (w_main, w_small, gdn_conv_w[l].astype(F32), prow, prow.T,
               gdn_norm_w[l].reshape(1, hd).astype(F32),
               w_out_b, w_out_b,
               ln1_g[l].reshape(1, -1), ln1_b[l].reshape(1, -1),
               peer_w_q[l].T.astype(BF16), peer_sub_keys[l].astype(BF16),
               peer_u[l].astype(BF16), peer_v[l].T.astype(BF16),
               ln2_g[l].reshape(1, -1), ln2_b[l].reshape(1, -1), ah, bh, hd, topk)
        conv0 = jnp.zeros((n_p, 3, 3 * aw), yp.dtype)
        s0 = jnp.zeros((n_p, ah, hd, hd), yp.dtype)
        yp, st_p = _trunk(yp, conv0, s0, None, wts, depth)
        ys, st_s = _trunk(ys, state_gdn_conv[l], state_gdn[l],
                          (cache_fox_k[l], cache_fox_v[l], cache_fox_logf[l]), wts, depth)
        outs_p.append(st_p)
        outs_s.append(st_s)
    stack = lambda outs, n: jnp.stack([o[n] for o in outs], axis=0)
    return ((yp, ys) + tuple(stack(outs_p, n) for n in range(5))
            + tuple(stack(outs_s, n) for n in range(5)))
```

```python
import functools

import jax
import jax.numpy as jnp
from jax import lax
from jax.experimental import pallas as pl
from jax.experimental.pallas import tpu as pltpu

F32 = jnp.float32
BF16 = jnp.bfloat16
LANE = 128
LN_EPS = 1e-5
RMS_EPS = 1e-6
L2_EPS = 1e-6
CONV_PAD = 8
VMEM_LIMIT = 58 * 1024 * 1024
HIGHEST = lax.Precision.HIGHEST
NEG_INF = float("-inf")
LOG2E = 1.4426950408889634
PEER_DRAIN_TRIPS = 16


def _params(sem):
    return pltpu.CompilerParams(dimension_semantics=sem, vmem_limit_bytes=VMEM_LIMIT)


def _tile(n, pref, align=LANE):
    if n <= pref:
        return n
    t = (pref // align) * align
    while t >= align:
        if n % t == 0:
            return t
        t -= align
    return n


def _dot(a, b):
    return jnp.dot(a, b, preferred_element_type=F32)


def _dot_nt(a, b):
    return lax.dot_general(a, b, (((1,), (1,)), ((), ())), preferred_element_type=F32)


def _dot_tn(a, b):
    return lax.dot_general(a, b, (((0,), (0,)), ((), ())), preferred_element_type=F32)


def _sigmoid(x):
    return 1.0 / (1.0 + jnp.exp(-x))


def _softplus(x):
    return jnp.maximum(x, 0.0) + jnp.log1p(jnp.exp(-jnp.abs(x)))


def _silu(x):
    return x * _sigmoid(x)


def _inproj_kernel(j_q, j_k, j_v, emit_t, x_ref, w_ref, ws_ref, o_ref, k_ref, v_ref, s_ref, st_ref, *rest):
    xb_ref = rest[-1]
    j = pl.program_id(1)

    @pl.when(j == 0)
    def _():
        xb = x_ref[...].astype(BF16)
        xb_ref[...] = xb
        sm = _dot(xb, ws_ref[...])
        s_ref[...] = sm
        st_ref[...] = sm.T

    @pl.when(j < j_q)
    def _():
        o_ref[...] = _dot(xb_ref[...], w_ref[...])

    @pl.when((j >= j_q) & (j < j_k))
    def _():
        acc = _dot(xb_ref[...], w_ref[...])
        o_ref[...] = acc
        if emit_t:
            rest[0][...] = acc.T.astype(BF16)

    @pl.when((j >= j_k) & (j < j_v))
    def _():
        k_ref[...] = _dot(xb_ref[...], w_ref[...])

    @pl.when(j >= j_v)
    def _():
        acc = _dot(xb_ref[...], w_ref[...])
        v_ref[...] = acc
        if emit_t:
            rest[0][...] = acc.T.astype(BF16)


def _inproj(x2d, w_main, w_small, n_a, bw, emit_t):
    m, d = x2d.shape
    n = w_main.shape[1]
    assert n == n_a + 3 * bw
    tm = _tile(m, 512)
    tn = _tile(bw, 1024)
    assert n_a % tn == 0 and bw % tn == 0
    nb = bw // tn
    j_q = n_a // tn
    j_k = j_q + nb
    j_v = j_k + nb
    out_specs = [
        pl.BlockSpec((tm, tn), lambda i, j: (i, jnp.minimum(j, j_k - 1))),
        pl.BlockSpec((tm, tn), lambda i, j: (i, jnp.clip(j - j_k, 0, nb - 1))),
        pl.BlockSpec((tm, tn), lambda i, j: (i, jnp.maximum(j - j_v, 0))),
        pl.BlockSpec((tm, LANE), lambda i, j: (i, 0)),
        pl.BlockSpec((LANE, tm), lambda i, j: (0, i)),
    ]
    out_shape = [
        jax.ShapeDtypeStruct((m, n_a + bw), F32),
        jax.ShapeDtypeStruct((m, bw), F32),
        jax.ShapeDtypeStruct((m, bw), F32),
        jax.ShapeDtypeStruct((m, LANE), F32),
        jax.ShapeDtypeStruct((LANE, m), F32),
    ]
    if emit_t:
        row_blk = lambda j: jnp.where(j < j_k, jnp.maximum(j - j_q, 0),
                                      jnp.where(j < j_v, nb - 1, j - j_v + nb))
        out_specs.append(pl.BlockSpec((tn, tm), lambda i, j: (row_blk(j), i)))
        out_shape.append(jax.ShapeDtypeStruct((2 * bw, m), BF16))
    return pl.pallas_call(
        functools.partial(_inproj_kernel, j_q, j_k, j_v, emit_t),
        grid=(m // tm, n // tn),
        in_specs=[
            pl.BlockSpec((tm, d), lambda i, j: (i, 0)),
            pl.BlockSpec((d, tn), lambda i, j: (0, j)),
            pl.BlockSpec((d, LANE), lambda i, j: (0, 0)),
        ],
        out_specs=out_specs,
        out_shape=out_shape,
        scratch_shapes=[pltpu.VMEM((tm, d), BF16)],
        compiler_params=_params(("parallel", "arbitrary")),
        name="inproj",
    )(x2d, w_main, w_small)


def _gate_values(z, a_log, bias, idx, ah, bh):
    zz = z + bias
    g = -jnp.exp(a_log) * _softplus(zz)
    beta = _sigmoid(zz)
    logf = -_softplus(-zz)
    return jnp.where(idx < ah, g, jnp.where(idx < 2 * ah, beta, jnp.where(idx < 2 * ah + 2 * bh, logf, 0.0)))


def _gate_merge(idx, cs, y, carry, ah, bh):
    return jnp.where(idx < ah, cs,
                     jnp.where(idx < 2 * ah, y,
                               jnp.where(idx < 2 * ah + bh, cs + carry,
                                         jnp.where(idx < 2 * ah + 2 * bh, y, 0.0))))


def _gates_kernel(apply, c_len, ah, bh, sm_ref, smt_ref, cc_ref, cr_ref, prow_ref, pcol_ref,
                  col_ref, rowc_ref, rowf_ref):
    t_len = sm_ref.shape[1]
    nc = t_len // c_len
    ii = lax.broadcasted_iota(jnp.int32, (c_len, c_len), 0)
    jj = lax.broadcasted_iota(jnp.int32, (c_len, c_len), 1)
    tril = (ii >= jj).astype(F32)
    triu = (ii <= jj).astype(F32)
    lane = lax.broadcasted_iota(jnp.int32, (c_len, LANE), 1)
    subl = lax.broadcasted_iota(jnp.int32, (LANE, c_len), 0)
    carry_c = cc_ref[0]
    carry_r = cr_ref[0]
    for c in range(nc):
        z = sm_ref[0, c * c_len:(c + 1) * c_len, :]
        y = _gate_values(z, prow_ref[0:1, :], prow_ref[1:2, :], lane, ah, bh) if apply else z
        cs = jnp.dot(tril, y, precision=HIGHEST, preferred_element_type=F32)
        col_ref[0, c * c_len:(c + 1) * c_len, :] = _gate_merge(lane, cs, y, carry_c, ah, bh)
        carry_c = carry_c + cs[c_len - 1:c_len, :]

        zt = smt_ref[0, :, c * c_len:(c + 1) * c_len]
        yt = _gate_values(zt, pcol_ref[:, 0:1], pcol_ref[:, 1:2], subl, ah, bh) if apply else zt
        cst = jnp.dot(yt, triu, precision=HIGHEST, preferred_element_type=F32)
        out_t = _gate_merge(subl, cst, yt, carry_r, ah, bh)
        rowc_ref[0, c] = out_t
        rowf_ref[0, :, c * c_len:(c + 1) * c_len] = out_t
        carry_r = carry_r + cst[:, c_len - 1:c_len]


def _gates(sm3, smt, carry_col, carry_row, prow, pcol, *, apply, c_len, ah, bh):
    b, t_len, _ = sm3.shape
    nc = t_len // c_len
    return pl.pallas_call(
        functools.partial(_gates_kernel, apply, c_len, ah, bh),
        grid=(b,),
        in_specs=[
            pl.BlockSpec((1, t_len, LANE), lambda i: (i, 0, 0)),
            pl.BlockSpec((1, LANE, t_len), lambda i: (i, 0, 0)),
            pl.BlockSpec((1, 1, LANE), lambda i: (i, 0, 0)),
            pl.BlockSpec((1, LANE, 1), lambda i: (i, 0, 0)),
            pl.BlockSpec((2, LANE), lambda i: (0, 0)),
            pl.BlockSpec((LANE, 2), lambda i: (0, 0)),
        ],
        out_specs=[
            pl.BlockSpec((1, t_len, LANE), lambda i: (i, 0, 0)),
            pl.BlockSpec((1, nc, LANE, c_len), lambda i: (i, 0, 0, 0)),
            pl.BlockSpec((1, LANE, t_len), lambda i: (i, 0, 0)),
        ],
        out_shape=[
            jax.ShapeDtypeStruct((b, t_len, LANE), F32),
            jax.ShapeDtypeStruct((b, nc, LANE, c_len), F32),
            jax.ShapeDtypeStruct((b, LANE, t_len), F32),
        ],
        compiler_params=_params(("parallel",)),
        name="gates",
    )(sm3, smt, carry_col, carry_row, prow, pcol)


def _unit_lower_inverses(lows, eye, ii, jj):
    c_len = lows[0].shape[0]
    same0 = (ii >> 1) == (jj >> 1)
    ts = [eye - jnp.where(same0, low, 0.0) for low in lows]
    k = 1
    while (2 << k) <= c_len:
        sel = ((ii >> (k + 1)) == (jj >> (k + 1))) & (((ii >> k) & 1) == 1) & (((jj >> k) & 1) == 0)
        tbs = [t.astype(BF16) for t in ts]
        mid = [_dot(tb, jnp.where(sel, low, 0.0).astype(BF16)).astype(BF16) for tb, low in zip(tbs, lows)]
        ts = [t - _dot(m, tb) for t, m, tb in zip(ts, mid, tbs)]
        k += 1
    return ts


def _gdn_kernel(c_len, hb, ah, q_ref, k_ref, v_ref, z_ref, hq_ref, hk_ref, hv_ref,
                cq_ref, ck_ref, cv_ref, col_ref, rowc_ref, s0_ref, nw_ref,
                o_ref, sn_ref, s_scr, buf_scr):
    c_idx = pl.program_id(2)
    hd = nw_ref.shape[1]
    lo = CONV_PAD - 3

    @pl.when(c_idx == 0)
    def _():
        s_scr[...] = s0_ref[0]
        buf_scr[0, lo:CONV_PAD, :] = hq_ref[0]
        buf_scr[1, lo:CONV_PAD, :] = hk_ref[0]
        buf_scr[2, lo:CONV_PAD, :] = hv_ref[0]

    conv = []
    for n, (x_ref, w_ref) in enumerate(((q_ref, cq_ref), (k_ref, ck_ref), (v_ref, cv_ref))):
        buf_scr[n, CONV_PAD:CONV_PAD + c_len, :] = x_ref[0]
        acc = w_ref[0:1, :] * buf_scr[n, lo:lo + c_len, :]
        for w in range(1, 4):
            acc = acc + w_ref[w:w + 1, :] * buf_scr[n, lo + w:lo + w + c_len, :]
        buf_scr[n, lo:CONV_PAD, :] = buf_scr[n, lo + c_len:CONV_PAD + c_len, :]
        conv.append(_silu(acc))
    qc, kc, vc = conv

    colblk = col_ref[0]
    ii = lax.broadcasted_iota(jnp.int32, (c_len, c_len), 0)
    jj = lax.broadcasted_iota(jnp.int32, (c_len, c_len), 1)
    eye = (ii == jj).astype(F32)
    nw = nw_ref[...]
    heads = range(hb)
    sls = [slice(h * hd, (h + 1) * hd) for h in heads]

    cum_c = [colblk[:, h:h + 1] for h in heads]
    beta_c = [colblk[:, ah + h:ah + h + 1] for h in heads]
    cum_r = [rowc_ref[0, 0, h:h + 1, :] for h in heads]
    cum_last = [r[:, c_len - 1:c_len] for r in cum_r]
    e_cum = [jnp.exp(c) for c in cum_c]

    q = [qc[:, sl] for sl in sls]
    k = [kc[:, sl] for sl in sls]
    q = [x * (lax.rsqrt(jnp.sum(x * x, axis=1, keepdims=True) + L2_EPS) * (hd ** -0.5)) for x in q]
    k = [x * lax.rsqrt(jnp.sum(x * x, axis=1, keepdims=True) + L2_EPS) for x in k]
    qb = [x.astype(BF16) for x in q]
    kb = [x.astype(BF16) for x in k]

    decay = [jnp.exp(jnp.where(ii >= jj, c - r, NEG_INF)) for c, r in zip(cum_c, cum_r)]
    kk = [_dot_nt(x, x) for x in kb]
    qk = [_dot_nt(x, y) for x, y in zip(qb, kb)]
    lows = [jnp.where(ii > jj, b * m * d, 0.0) for b, m, d in zip(beta_c, kk, decay)]
    t_inv = _unit_lower_inverses(lows, eye, ii, jj)

    rhs = [jnp.concatenate([b * vc[:, sl], (b * e) * x], axis=1).astype(BF16)
           for b, e, x, sl in zip(beta_c, e_cum, k, sls)]
    w_all = [_dot(t.astype(BF16), r) for t, r in zip(t_inv, rhs)]

    s_old = [s_scr[h] for h in heads]
    sb = [s.astype(BF16) for s in s_old]
    u = [w[:, :hd] - _dot(w[:, hd:].astype(BF16), s) for w, s in zip(w_all, sb)]
    ub = [x.astype(BF16) for x in u]
    o = [_dot((x * e).astype(BF16), s) + _dot((m * d).astype(BF16), y)
         for x, e, s, m, d, y in zip(q, e_cum, sb, qk, decay, ub)]
    for h in heads:
        k_dec = k[h] * jnp.exp(cum_last[h] - cum_c[h])
        s_scr[h] = s_old[h] * jnp.exp(cum_last[h]) + _dot_tn(k_dec.astype(BF16), ub[h])
    for h in heads:
        x = o[h] * lax.rsqrt(jnp.mean(o[h] * o[h], axis=1, keepdims=True) + RMS_EPS) * nw
        o_ref[0, :, sls[h]] = (x * _silu(z_ref[0, :, sls[h]])).astype(o_ref.dtype)

    @pl.when(c_idx == pl.num_programs(2) - 1)
    def _():
        sn_ref[0] = s_scr[...]


def _gdn(proj3, conv_hist, conv_w, col, rowc, s0, norm_w, *, c_len, ah, hd):
    b, t_len, _ = proj3.shape
    nc = t_len // c_len
    aw = ah * hd
    hb = ah
    ng = ah // hb
    wb = hb * hd
    kq, kk_, kv, kz = 0, ng, 2 * ng, 3 * ng
    tok = lambda off: pl.BlockSpec((1, c_len, wb), lambda i, g, c: (i, c, off + g))
    hist = lambda off: pl.BlockSpec((1, 3, wb), lambda i, g, c: (i, 0, off + g))
    cw = lambda off: pl.BlockSpec((4, wb), lambda i, g, c: (0, off + g))
    return pl.pallas_call(
        functools.partial(_gdn_kernel, c_len, hb, ah),
        grid=(b, ng, nc),
        in_specs=[
            tok(kq), tok(kk_), tok(kv), tok(kz),
            hist(kq), hist(kk_), hist(kv),
            cw(kq), cw(kk_), cw(kv),
            pl.BlockSpec((1, c_len, LANE), lambda i, g, c: (i, c, 0)),
            pl.BlockSpec((1, 1, LANE, c_len), lambda i, g, c: (i, c, 0, 0)),
            pl.BlockSpec((1, hb, hd, hd), lambda i, g, c: (i, g, 0, 0)),
            pl.BlockSpec((1, hd), lambda i, g, c: (0, 0)),
        ],
        out_specs=[
            pl.BlockSpec((1, c_len, wb), lambda i, g, c: (i, c, g)),
            pl.BlockSpec((1, hb, hd, hd), lambda i, g, c: (i, g, 0, 0)),
        ],
        out_shape=[
            jax.ShapeDtypeStruct((b, t_len, aw), BF16),
            jax.ShapeDtypeStruct((b, ah, hd, hd), F32),
        ],
        scratch_shapes=[
            pltpu.VMEM((hb, hd, hd), F32),
            pltpu.VMEM((3, CONV_PAD + c_len, wb), F32),
        ],
        compiler_params=_params(("parallel", "parallel", "arbitrary")),
        name="gdn",
    )(proj3, proj3, proj3, proj3, conv_hist, conv_hist, conv_hist, conv_w, conv_w, conv_w,
      col, rowc, s0, norm_w)


def _fox_s_kernel(f_lane, bh, hd, q_ref, kn_ref, vn_ref, kc_ref, vc_ref, cq_ref, ckc_ref, ckn_ref, o_ref):
    t_q = q_ref.shape[1]
    p_len = kc_ref.shape[1] // bh
    heads = range(bh)
    sls = [slice(h * hd, (h + 1) * hd) for h in heads]
    q_scale = (hd ** -0.5) * LOG2E
    cqb = cq_ref[0] * LOG2E
    keep = (lax.broadcasted_iota(jnp.int32, (t_q, t_q), 1) <= lax.broadcasted_iota(jnp.int32, (t_q, t_q), 0))
    cq = [cqb[:, f_lane + h:f_lane + h + 1] for h in heads]
    ckc = [ckc_ref[0, f_lane + h:f_lane + h + 1, :] * LOG2E for h in heads]
    ckn = [ckn_ref[0, f_lane + h:f_lane + h + 1, :] * LOG2E for h in heads]
    qb = [(q_ref[0, :, sl] * q_scale).astype(BF16) for sl in sls]
    kc = [kc_ref[0, pl.ds(h, p_len, stride=bh), :].astype(BF16) for h in heads]
    xc = [_dot_nt(a, b) - c for a, b, c in zip(qb, kc, ckc)]
    xn = [jnp.where(keep, _dot_nt(a, kn_ref[0, :, sl].astype(BF16)) - c, NEG_INF)
          for a, sl, c in zip(qb, sls, ckn)]
    m = [c + jnp.maximum(jnp.max(a, axis=1, keepdims=True), jnp.max(b, axis=1, keepdims=True))
         for a, b, c in zip(xc, xn, cq)]
    pc = [jnp.exp2(a + (c - mm)) for a, c, mm in zip(xc, cq, m)]
    pn = [jnp.exp2(a + (c - mm)) for a, c, mm in zip(xn, cq, m)]
    vc = [vc_ref[0, pl.ds(h, p_len, stride=bh), :].astype(BF16) for h in heads]
    acc = [_dot(a.astype(BF16), b) + _dot(c.astype(BF16), vn_ref[0, :, sl].astype(BF16))
           for a, b, c, sl in zip(pc, vc, pn, sls)]
    for h in heads:
        den = jnp.sum(pc[h], axis=1, keepdims=True) + jnp.sum(pn[h], axis=1, keepdims=True)
        o_ref[0, :, sls[h]] = (acc[h] / den).astype(o_ref.dtype)


def _fox_s(q_arr, q_blk, k_new, v_new, k_cache, v_cache, cq_col, ck_cache_row, ck_new_row, *, bh, hd, f_lane):
    b, t_q, bw = k_new.shape
    rows = k_cache.shape[1]
    p_len = rows // bh
    return pl.pallas_call(
        functools.partial(_fox_s_kernel, f_lane, bh, hd),
        grid=(b,),
        in_specs=[
            pl.BlockSpec((1, t_q, bw), lambda i: (i, 0, q_blk)),
            pl.BlockSpec((1, t_q, bw), lambda i: (i, 0, 0)),
            pl.BlockSpec((1, t_q, bw), lambda i: (i, 0, 0)),
            pl.BlockSpec((1, rows, hd), lambda i: (i, 0, 0)),
            pl.BlockSpec((1, rows, hd), lambda i: (i, 0, 0)),
            pl.BlockSpec((1, t_q, LANE), lambda i: (i, 0, 0)),
            pl.BlockSpec((1, LANE, p_len), lambda i: (i, 0, 0)),
            pl.BlockSpec((1, LANE, t_q), lambda i: (i, 0, 0)),
        ],
        out_specs=pl.BlockSpec((1, t_q, bw), lambda i: (i, 0, 0)),
        out_shape=jax.ShapeDtypeStruct((b, t_q, bw), BF16),
        compiler_params=_params(("parallel",)),
        name="fox_s",
    )(q_arr, k_new, v_new, k_cache, v_cache, cq_col, ck_cache_row, ck_new_row)


def _fox_t_kernel(f_lane, hg, hd, k_ref, qt_ref, vt_ref, cq_ref, ck_ref, o_ref, m_scr, l_scr, acc_scr):
    g = pl.program_id(1)
    qi = pl.program_id(2)
    kj = pl.program_id(3)
    tk = k_ref.shape[1]
    tq = qt_ref.shape[1]
    heads = range(hg)
    sls = [slice(h * hd, (h + 1) * hd) for h in heads]
    k_scale = (hd ** -0.5) * LOG2E

    @pl.when(kj == 0)
    def _():
        m_scr[...] = jnp.full(m_scr.shape, NEG_INF, F32)
        l_scr[...] = jnp.zeros(l_scr.shape, F32)
        acc_scr[...] = jnp.zeros(acc_scr.shape, F32)

    def update(masked):
        ckb = ck_ref[0] * LOG2E
        lane = lax.broadcasted_iota(jnp.int32, ckb.shape, 1)
        ck = [jnp.sum(jnp.where(lane == f_lane + g * hg + h, ckb, 0.0), axis=1, keepdims=True)
              for h in heads]
        cq = [cq_ref[0, pl.ds(f_lane + g * hg + h, 1), :] * LOG2E for h in heads]
        kb = [(k_ref[0, :, sl] * k_scale).astype(BF16) for sl in sls]
        x = [_dot(a, qt_ref[sl, :]) - c for a, sl, c in zip(kb, sls, ck)]
        if masked:
            kpos = kj * tk + lax.broadcasted_iota(jnp.int32, (tk, tq), 0)
            qpos = qi * tq + lax.broadcasted_iota(jnp.int32, (tk, tq), 1)
            keep = kpos <= qpos
            x = [jnp.where(keep, a, NEG_INF) for a in x]
        m_old = [m_scr[h] for h in heads]
        m_new = [jnp.maximum(mo, c + jnp.max(a, axis=0, keepdims=True)) for mo, c, a in zip(m_old, cq, x)]
        p = [jnp.exp2(a + (c - mn)) for a, c, mn in zip(x, cq, m_new)]
        alpha = [jnp.exp2(mo - mn) for mo, mn in zip(m_old, m_new)]
        pv = [_dot(vt_ref[sl, :], a.astype(BF16)) for a, sl in zip(p, sls)]
        for h in heads:
            l_scr[h] = alpha[h] * l_scr[h] + jnp.sum(p[h], axis=0, keepdims=True)
            acc_scr[h] = alpha[h] * acc_scr[h] + pv[h]
            m_scr[h] = m_new[h]

    active = kj * tk <= qi * tq + (tq - 1)
    crosses = kj * tk + (tk - 1) > qi * tq

    @pl.when(active & crosses)
    def _():
        update(True)

    @pl.when(active & jnp.logical_not(crosses))
    def _():
        update(False)

    @pl.when(kj == pl.num_programs(3) - 1)
    def _():
        for h in heads:
            o_ref[0, :, sls[h]] = (acc_scr[h] / l_scr[h]).T.astype(o_ref.dtype)


def _fox_t(k3, qv_t, cq_row, ck_col, *, bh, hd, f_lane):
    b, t_len, bw = k3.shape
    tq = _tile(t_len, 512)
    tk = tq
    nq = t_len // tq
    hg = 8 if bh % 8 == 0 else bh
    wg = hg * hd
    ng = bh // hg
    last = lambda qi: (qi * tq + (tq - 1)) // tk
    kblk = lambda qi, kj: jnp.minimum(kj, last(qi))
    return pl.pallas_call(
        functools.partial(_fox_t_kernel, f_lane, hg, hd),
        grid=(b, ng, nq, nq),
        in_specs=[
            pl.BlockSpec((1, tk, wg), lambda i, g, qi, kj: (i, kblk(qi, kj), g)),
            pl.BlockSpec((wg, tq), lambda i, g, qi, kj: (g, i * nq + qi)),
            pl.BlockSpec((wg, tk), lambda i, g, qi, kj: (ng + g, i * nq + kblk(qi, kj))),
            pl.BlockSpec((1, LANE, tq), lambda i, g, qi, kj: (i, 0, qi)),
            pl.BlockSpec((1, tk, LANE), lambda i, g, qi, kj: (i, kblk(qi, kj), 0)),
        ],
        out_specs=pl.BlockSpec((1, tq, wg), lambda i, g, qi, kj: (i, qi, g)),
        out_shape=jax.ShapeDtypeStruct((b, t_len, bw), BF16),
        scratch_shapes=[
            pltpu.VMEM((hg, 1, tq), F32),
            pltpu.VMEM((hg, 1, tq), F32),
            pltpu.VMEM((hg, hd, tq), F32),
        ],
        compiler_params=_params(("parallel", "parallel", "parallel", "arbitrary")),
        name="fox_t",
    )(k3, qv_t, qv_t, cq_row, ck_col)


def _layer_norm_rows(x, g, b):
    mu = jnp.mean(x, axis=1, keepdims=True)
    xc = x - mu
    var = jnp.mean(xc * xc, axis=1, keepdims=True)
    return xc * lax.rsqrt(var + LN_EPS) * g + b


def _layer_norm_ref(ref, g_ref, b_ref, rows):
    def body(r, carry):
        sl = pl.ds(pl.multiple_of(r * rows, rows), rows)
        ref[sl, :] = _layer_norm_rows(ref[sl, :], g_ref[...], b_ref[...])
        return carry
    lax.fori_loop(0, ref.shape[0] // rows, body, 0)


def _outproj_kernel(alpha, tn, oa_ref, ob_ref, wa_ref, wb_ref, x_ref, g_ref, b_ref, hid_ref, hidt_ref):
    j = pl.program_id(1)
    col = pl.multiple_of(j * tn, LANE)
    hid_ref[:, pl.ds(col, tn)] = (alpha * x_ref[...] + _dot(oa_ref[...], wa_ref[...])
                                  + _dot(ob_ref[...], wb_ref[...]))

    @pl.when(j == pl.num_programs(1) - 1)
    def _():
        tm, d = hid_ref.shape
        _layer_norm_ref(hid_ref, g_ref, b_ref, min(tm, 32))
        rb = min(tm, LANE)
        step = _tile(d, 512)
        for r in range(tm // rb):
            for c in range(d // step):
                hidt_ref[c * step:(c + 1) * step, r * rb:(r + 1) * rb] = (
                    hid_ref[r * rb:(r + 1) * rb, c * step:(c + 1) * step].T.astype(BF16))


def _outproj(o_a, o_b, w_a, w_b, x2d, g, b, alpha):
    m, d = x2d.shape
    tm = _tile(m, 512)
    tn = _tile(d, 512)
    aw, bw = o_a.shape[1], o_b.shape[1]
    assert w_a.shape[0] == aw + bw and aw % bw == 0
    return pl.pallas_call(
        functools.partial(_outproj_kernel, alpha, tn),
        grid=(m // tm, d // tn),
        in_specs=[
            pl.BlockSpec((tm, o_a.shape[1]), lambda i, j: (i, 0)),
            pl.BlockSpec((tm, o_b.shape[1]), lambda i, j: (i, 0)),
            pl.BlockSpec((aw, tn), lambda i, j: (0, j)),
            pl.BlockSpec((bw, tn), lambda i, j: (aw // bw, j)),
            pl.BlockSpec((tm, tn), lambda i, j: (i, j)),
            pl.BlockSpec((1, d), lambda i, j: (0, 0)),
            pl.BlockSpec((1, d), lambda i, j: (0, 0)),
        ],
        out_specs=[
            pl.BlockSpec((tm, d), lambda i, j: (i, 0)),
            pl.BlockSpec((d, tm), lambda i, j: (0, i)),
        ],
        out_shape=[
            jax.ShapeDtypeStruct((m, d), F32),
            jax.ShapeDtypeStruct((d, m), BF16),
        ],
        compiler_params=_params(("parallel", "arbitrary")),
        name="outproj",
    )(o_a, o_b, w_a, w_b, x2d, g, b)


def _top_values(x, n):
    vals = []
    for _ in range(n):
        m = jnp.max(x, axis=0, keepdims=True)
        vals.append(m)
        x = jnp.where(x == m, NEG_INF, x)
    return vals


def _route_kernel(topk, hp, ht_ref, wq_ref, key_ref, a1_ref, a2_ref, thr_ref, cand_scr):
    dk = key_ref.shape[3]
    n = topk + 1
    pairs = [(a, b) for a in range(n) for b in range(n) if (a + 1) * (b + 1) <= n]

    def scores(h):
        qt = _dot(wq_ref[h * 2 * dk:(h + 1) * 2 * dk, :], ht_ref[...])
        return (_dot(key_ref[h, 0], qt[:dk].astype(BF16)),
                _dot(key_ref[h, 1], qt[dk:].astype(BF16)))

    nxt = scores(0)
    for h in range(hp):
        s1, s2 = nxt
        if h + 1 < hp:
            nxt = scores(h + 1)
        top1 = _top_values(s1, n)
        top2 = _top_values(s2, n)
        cand_scr[...] = jnp.full(cand_scr.shape, NEG_INF, F32)
        for r, (a, b) in enumerate(pairs):
            cand_scr[r:r + 1, :] = top1[a] + top2[b]
        best = _top_values(cand_scr[...], n)
        z = jnp.exp(best[0] - best[0])
        for t in best[1:topk]:
            z = z + jnp.exp(t - best[0])
        a1_ref[h] = jnp.exp(s1 - top1[0]) / z
        a2_ref[h] = jnp.exp(s2 - top2[0])
        thr = jnp.exp(0.5 * (best[topk - 1] + best[topk]) - best[0]) / z
        thr_ref[h] = jnp.broadcast_to(thr, thr_ref.shape[1:])


def _num_candidates(n):
    return -(-sum(n // (a + 1) for a in range(n)) // 8) * 8


def _route(hid_t, wq_t, keys, topk):
    d, m = hid_t.shape
    ph, _, nk, dk = keys.shape
    tm = _tile(m, 512)
    hp = 4 if ph % 4 == 0 else ph
    out = jax.ShapeDtypeStruct((ph, nk, m), F32)
    blk = pl.BlockSpec((hp, nk, tm), lambda i, h: (h, 0, i))
    return pl.pallas_call(
        functools.partial(_route_kernel, topk, hp),
        grid=(m // tm, ph // hp),
        in_specs=[
            pl.BlockSpec((d, tm), lambda i, h: (0, i)),
            pl.BlockSpec((hp * 2 * dk, d), lambda i, h: (h, 0)),
            pl.BlockSpec((hp, 2, nk, dk), lambda i, h: (h, 0, 0, 0)),
        ],
        out_specs=[blk, blk, pl.BlockSpec((hp, 8, tm), lambda i, h: (h, 0, i))],
        out_shape=[out, out, jax.ShapeDtypeStruct((ph, 8, m), F32)],
        scratch_shapes=[pltpu.VMEM((_num_candidates(topk + 1), tm), F32)],
        compiler_params=_params(("parallel", "arbitrary")),
        name="peer_route",
    )(hid_t, wq_t, keys)


def _peer_kernel(ht_ref, u_ref, vt_ref, a1_ref, a2_ref, thr_ref, o_ref, act_scr, pre_scr):
    e = pl.program_id(1)
    n_tiles = pl.num_programs(1) - 1
    ph, nk, tm = a2_ref.shape
    te = u_ref.shape[0]
    n_sub = te // nk
    d = o_ref.shape[0]
    slot = e % 2
    strip = min(tm, LANE)

    @pl.when(e == 0)
    def _():
        o_ref[...] = jnp.zeros(o_ref.shape, F32)
        act_scr[1] = jnp.zeros(act_scr.shape[1:], BF16)

    @pl.when(e < n_tiles)
    def _():
        pre_scr[...] = _dot(u_ref[...], ht_ref[...])

    n_ch = PEER_DRAIN_TRIPS
    step = d // n_ch
    gr = te // n_ch
    for c in range(n_ch):
        sl = slice(c * step, (c + 1) * step)
        o_ref[sl, :] += _dot(vt_ref[sl, :], act_scr[1 - slot])
        for r0 in range(c * gr, (c + 1) * gr, min(gr, nk)):
            rn = min(gr, nk)
            row = jnp.minimum(e * n_sub + r0 // nk, nk - 1)
            j0 = r0 % nk
            a1_rows = [a1_ref[h, pl.ds(row, 1), :] for h in range(ph)]
            for t in range(tm // strip):
                ls = slice(t * strip, (t + 1) * strip)
                gate = None
                for h in range(ph):
                    g = a2_ref[h, j0:j0 + rn, ls] * a1_rows[h][:, ls]
                    term = jnp.where(g >= thr_ref[h, 0:1, ls], g, 0.0)
                    gate = term if gate is None else gate + term
                x = pre_scr[r0:r0 + rn, ls]
                act = gate * (0.5 * x * (1.0 + lax.erf(x * (2.0 ** -0.5))))
                act_scr[slot, r0:r0 + rn, ls] = act.astype(BF16)


def _peer(hid_t, u_b, v_t, a1, a2, thr):
    d, m = hid_t.shape
    ne = u_b.shape[0]
    ph, nk, _ = a1.shape
    tm = _tile(m, 512)
    te = _tile(ne, 512)
    n_tiles = ne // te
    assert te % nk == 0 and d % (te // nk) == 0
    rt = pl.BlockSpec((ph, nk, tm), lambda i, e: (0, 0, i))
    return pl.pallas_call(
        _peer_kernel,
        grid=(m // tm, n_tiles + 1),
        in_specs=[
            pl.BlockSpec((d, tm), lambda i, e: (0, i)),
            pl.BlockSpec((te, d), lambda i, e: (jnp.minimum(e, n_tiles - 1), 0)),
            pl.BlockSpec((d, te), lambda i, e: (0, jnp.maximum(e - 1, 0))),
            rt, rt,
            pl.BlockSpec((ph, 8, tm), lambda i, e: (0, 0, i)),
        ],
        out_specs=pl.BlockSpec((d, tm), lambda i, e: (0, i)),
        out_shape=jax.ShapeDtypeStruct((d, m), F32),
        scratch_shapes=[pltpu.VMEM((2, te, tm), BF16), pltpu.VMEM((te, tm), F32)],
        compiler_params=_params(("parallel", "arbitrary")),
        name="peer_dense",
    )(hid_t, u_b, v_t, a1, a2, thr)


def _final_kernel(alpha, hid_ref, pt_ref, g_ref, b_ref, y_ref):
    d = hid_ref.shape[1]
    step = _tile(d, 512)
    for c in range(d // step):
        sl = slice(c * step, (c + 1) * step)
        y_ref[:, sl] = alpha * hid_ref[:, sl] + pt_ref[sl, :].T
    _layer_norm_ref(y_ref, g_ref, b_ref, min(y_ref.shape[0], 32))


def _final(hid, peer_t, g, b, alpha):
    m, d = hid.shape
    tm = _tile(m, 256)
    return pl.pallas_call(
        functools.partial(_final_kernel, alpha),
        grid=(m // tm,),
        in_specs=[
            pl.BlockSpec((tm, d), lambda i: (i, 0)),
            pl.BlockSpec((d, tm), lambda i: (0, i)),
            pl.BlockSpec((1, d), lambda i: (0, 0)),
            pl.BlockSpec((1, d), lambda i: (0, 0)),
        ],
        out_specs=pl.BlockSpec((tm, d), lambda i: (i, 0)),
        out_shape=jax.ShapeDtypeStruct((m, d), F32),
        compiler_params=_params(("parallel",)),
        name="final_ln",
    )(hid, peer_t, g, b)


def _pad_lanes(a):
    return jnp.pad(a, [(0, 0)] * (a.ndim - 1) + [(0, LANE - a.shape[-1])])


def _trunk(x, conv_hist, s0, fox_cache, wts, depth):
    (w_main, w_small, conv_w, prow, pcol, norm_w, w_a, w_b, ln1_g, ln1_b, wq_t, keys, u_b, v_t,
     ln2_g, ln2_b, ah, bh, hd, topk) = wts
    b, t_len, d = x.shape
    m = b * t_len
    aw, bw = ah * hd, bh * hd
    alpha = (2 * depth) ** 0.25
    f_lane = 2 * ah
    x2d = x.reshape(m, d)

    n_a = 4 * aw
    prompt = fox_cache is None
    proj, k_new, v_new, sm, smt, *qv_t = _inproj(x2d, w_main, w_small, n_a, bw, prompt)
    proj3 = proj.reshape(b, t_len, -1)
    k_new = k_new.reshape(b, t_len, bw)
    v_new = v_new.reshape(b, t_len, bw)

    c_len = min(64, t_len)
    zero_c = jnp.zeros((b, 1, LANE), F32)
    zero_r = jnp.zeros((b, LANE, 1), F32)
    if fox_cache is None:
        carry_c, carry_r = zero_c, zero_r
    else:
        clf = fox_cache[2].astype(F32)
        p_len = clf.shape[1]
        clf_col = jnp.pad(clf, ((0, 0), (0, 0), (f_lane, LANE - f_lane - bh)))
        clf_row = jnp.swapaxes(clf_col, 1, 2)
        cc_col, _, cc_row = _gates(clf_col, clf_row, zero_c, zero_r, prow, pcol,
                                   apply=False, c_len=min(64, p_len), ah=ah, bh=bh)
        carry_c = cc_col[:, p_len - 1:, :]
        carry_r = cc_row[:, :, p_len - 1:]
    smt3 = jnp.swapaxes(smt.reshape(LANE, b, t_len), 0, 1)
    col, rowc, rowf = _gates(sm.reshape(b, t_len, LANE), smt3, carry_c, carry_r, prow, pcol,
                             apply=True, c_len=c_len, ah=ah, bh=bh)
    logf = col[:, :, f_lane + bh:f_lane + 2 * bh]

    o_a, s_new = _gdn(proj3, conv_hist, conv_w, col, rowc, s0, norm_w, c_len=c_len, ah=ah, hd=hd)
    conv_new = jnp.concatenate([conv_hist.astype(F32), proj3[:, :, :3 * aw]], axis=1)[:, -3:]

    qb0 = n_a // hd
    if prompt:
        o_b = _fox_t(k_new, qv_t[0], rowf, col, bh=bh, hd=hd, f_lane=f_lane)
    else:
        assert n_a % bw == 0
        k_cache = fox_cache[0].reshape(b, -1, hd).astype(F32)
        v_cache = fox_cache[1].reshape(b, -1, hd).astype(F32)
        o_b = _fox_s(proj3, n_a // bw, k_new, v_new, k_cache, v_cache, col, cc_row, rowf,
                     bh=bh, hd=hd, f_lane=f_lane)

    hid, hid_t = _outproj(o_a.reshape(m, aw), o_b.reshape(m, bw), w_a, w_b, x2d, ln1_g, ln1_b, alpha)
    a1, a2, thr = _route(hid_t, wq_t, keys, topk)
    peer_t = _peer(hid_t, u_b, v_t, a1, a2, thr)
    y = _final(hid, peer_t, ln2_g, ln2_b, alpha).reshape(b, t_len, d)
    return y, (k_new.reshape(b, t_len, bh, hd), v_new.reshape(b, t_len, bh, hd), logf, s_new, conv_new)


def kernel(x_prompt, x_sample, cache_fox_k, cache_fox_v, cache_fox_logf, state_gdn, state_gdn_conv,
           w_in, gdn_conv_w, gdn_a_log, gdn_dt_bias, gdn_norm_w, fox_f_bias, w_out, ln1_g, ln1_b,
           peer_w_q, peer_sub_keys, peer_u, peer_v, ln2_g, ln2_b):
    depth = w_in.shape[0]
    ah = gdn_a_log.shape[1]
    bh = fox_f_bias.shape[1]
    hd = gdn_norm_w.shape[1]
    aw, bw = ah * hd, bh * hd
    topk = 16
    assert 2 * ah + 2 * bh <= LANE
    n_p = x_prompt.shape[0]
    yp, ys = x_prompt, x_sample
    outs_p, outs_s = [], []
    for l in range(depth):
        o_a_a = 4 * aw
        o_b_qkv = o_a_a + 2 * ah
        o_b_f = o_b_qkv + 3 * bw
        wl = w_in[l]
        w_main = jnp.concatenate([wl[:, :o_a_a].astype(BF16), wl[:, o_b_qkv:o_b_f].astype(BF16)], axis=1)
        w_out_b = w_out[l].astype(BF16)
        w_f = wl[:, o_b_f:o_b_f + bh]
        w_small = _pad_lanes(jnp.concatenate([wl[:, o_a_a:o_b_qkv], w_f, w_f], axis=1)).astype(BF16)
        zeros_a = jnp.zeros((ah,), F32)
        prow = jnp.stack([
            _pad_lanes(gdn_a_log[l].astype(F32)),
            _pad_lanes(jnp.concatenate([gdn_dt_bias[l].astype(F32), zeros_a,
                                        fox_f_bias[l].astype(F32), fox_f_bias[l].astype(F32)])),
        ])
        wts = (w_main, w_small, gdn_conv_w[l].astype(F32), prow, prow.T,
               gdn_norm_w[l].reshape(1, hd).astype(F32),
               w_out_b, w_out_b,
               ln1_g[l].reshape(1, -1), ln1_b[l].reshape(1, -1),
               peer_w_q[l].T.astype(BF16), peer_sub_keys[l].astype(BF16),
               peer_u[l].astype(BF16), peer_v[l].T.astype(BF16),
               ln2_g[l].reshape(1, -1), ln2_b[l].reshape(1, -1), ah, bh, hd, topk)
        conv0 = jnp.zeros((n_p, 3, 3 * aw), yp.dtype)
        s0 = jnp.zeros((n_p, ah, hd, hd), yp.dtype)
        yp, st_p = _trunk(yp, conv0, s0, None, wts, depth)
        ys, st_s = _trunk(ys, state_gdn_conv[l], state_gdn[l],
                          (cache_fox_k[l], cache_fox_v[l], cache_fox_logf[l]), wts, depth)
        outs_p.append(st_p)
        outs_s.append(st_s)
    stack = lambda outs, n: jnp.stack([o[n] for o in outs], axis=0)
    return ((yp, ys) + tuple(stack(outs_p, n) for n in range(5))
            + tuple(stack(outs_s, n) for n in range(5)))
```

```python
import functools

import jax
import jax.numpy as jnp
from jax import lax
from jax.experimental import pallas as pl
from jax.experimental.pallas import tpu as pltpu

F32 = jnp.float32
BF16 = jnp.bfloat16
LANE = 128
LN_EPS = 1e-5
RMS_EPS = 1e-6
L2_EPS = 1e-6
CONV_PAD = 8
VMEM_LIMIT = 58 * 1024 * 1024
HIGHEST = lax.Precision.HIGHEST
NEG_INF = float("-inf")
LOG2E = 1.4426950408889634
PEER_DRAIN_TRIPS = 16


def _params(sem):
    return pltpu.CompilerParams(dimension_semantics=sem, vmem_limit_bytes=VMEM_LIMIT)


def _tile(n, pref, align=LANE):
    if n <= pref:
        return n
    t = (pref // align) * align
    while t >= align:
        if n % t == 0:
            return t
        t -= align
    return n


def _dot(a, b):
    return jnp.dot(a, b, preferred_element_type=F32)


def _dot_nt(a, b):
    return lax.dot_general(a, b, (((1,), (1,)), ((), ())), preferred_element_type=F32)


def _dot_tn(a, b):
    return lax.dot_general(a, b, (((0,), (0,)), ((), ())), preferred_element_type=F32)


def _sigmoid(x):
    return 1.0 / (1.0 + jnp.exp(-x))


def _softplus(x):
    return jnp.maximum(x, 0.0) + jnp.log1p(jnp.exp(-jnp.abs(x)))


def _silu(x):
    return x * _sigmoid(x)


def _inproj_kernel(j_q, j_k, j_v, emit_t, x_ref, w_ref, ws_ref, o_ref, k_ref, v_ref, s_ref, st_ref, *rest):
    xb_ref = rest[-1]
    j = pl.program_id(1)

    @pl.when(j == 0)
    def _():
        xb = x_ref[...].astype(BF16)
        xb_ref[...] = xb
        sm = _dot(xb, ws_ref[...])
        s_ref[...] = sm
        st_ref[...] = sm.T

    @pl.when(j < j_q)
    def _():
        o_ref[...] = _dot(xb_ref[...], w_ref[...])

    @pl.when((j >= j_q) & (j < j_k))
    def _():
        acc = _dot(xb_ref[...], w_ref[...])
        o_ref[...] = acc
        if emit_t:
            rest[0][...] = acc.T.astype(BF16)

    @pl.when((j >= j_k) & (j < j_v))
    def _():
        k_ref[...] = _dot(xb_ref[...], w_ref[...])

    @pl.when(j >= j_v)
    def _():
        acc = _dot(xb_ref[...], w_ref[...])
        v_ref[...] = acc
        if emit_t:
            rest[0][...] = acc.T.astype(BF16)


def _inproj(x2d, w_main, w_small, n_a, bw, emit_t):
    m, d = x2d.shape
    n = w_main.shape[1]
    assert n == n_a + 3 * bw
    tm = _tile(m, 512)
    tn = _tile(bw, 1024)
    assert n_a % tn == 0 and bw % tn == 0
    nb = bw // tn
    j_q = n_a // tn
    j_k = j_q + nb
    j_v = j_k + nb
    out_specs = [
        pl.BlockSpec((tm, tn), lambda i, j: (i, jnp.minimum(j, j_k - 1))),
        pl.BlockSpec((tm, tn), lambda i, j: (i, jnp.clip(j - j_k, 0, nb - 1))),
        pl.BlockSpec((tm, tn), lambda i, j: (i, jnp.maximum(j - j_v, 0))),
        pl.BlockSpec((tm, LANE), lambda i, j: (i, 0)),
        pl.BlockSpec((LANE, tm), lambda i, j: (0, i)),
    ]
    out_shape = [
        jax.ShapeDtypeStruct((m, n_a + bw), F32),
        jax.ShapeDtypeStruct((m, bw), F32),
        jax.ShapeDtypeStruct((m, bw), F32),
        jax.ShapeDtypeStruct((m, LANE), F32),
        jax.ShapeDtypeStruct((LANE, m), F32),
    ]
    if emit_t:
        row_blk = lambda j: jnp.where(j < j_k, jnp.maximum(j - j_q, 0),
                                      jnp.where(j < j_v, nb - 1, j - j_v + nb))
        out_specs.append(pl.BlockSpec((tn, tm), lambda i, j: (row_blk(j), i)))
        out_shape.append(jax.ShapeDtypeStruct((2 * bw, m), BF16))
    return pl.pallas_call(
        functools.partial(_inproj_kernel, j_q, j_k, j_v, emit_t),
        grid=(m // tm, n // tn),
        in_specs=[
            pl.BlockSpec((tm, d), lambda i, j: (i, 0)),
            pl.BlockSpec((d, tn), lambda i, j: (0, j)),
            pl.BlockSpec((d, LANE), lambda i, j: (0, 0)),
        ],
        out_specs=out_specs,
        out_shape=out_shape,
        scratch_shapes=[pltpu.VMEM((tm, d), BF16)],
        compiler_params=_params(("parallel", "arbitrary")),
        name="inproj",
    )(x2d, w_main, w_small)


def _gate_values(z, a_log, bias, idx, ah, bh):
    zz = z + bias
    g = -jnp.exp(a_log) * _softplus(zz)
    beta = _sigmoid(zz)
    logf = -_softplus(-zz)
    return jnp.where(idx < ah, g, jnp.where(idx < 2 * ah, beta, jnp.where(idx < 2 * ah + 2 * bh, logf, 0.0)))


def _gate_merge(idx, cs, y, carry, ah, bh):
    return jnp.where(idx < ah, cs,
                     jnp.where(idx < 2 * ah, y,
                               jnp.where(idx < 2 * ah + bh, cs + carry,
                                         jnp.where(idx < 2 * ah + 2 * bh, y, 0.0))))


def _gates_kernel(apply, c_len, ah, bh, sm_ref, smt_ref, cc_ref, cr_ref, prow_ref, pcol_ref,
                  col_ref, rowc_ref, rowf_ref):
    t_len = sm_ref.shape[1]
    nc = t_len // c_len
    ii = lax.broadcasted_iota(jnp.int32, (c_len, c_len), 0)
    jj = lax.broadcasted_iota(jnp.int32, (c_len, c_len), 1)
    tril = (ii >= jj).astype(F32)
    triu = (ii <= jj).astype(F32)
    lane = lax.broadcasted_iota(jnp.int32, (c_len, LANE), 1)
    subl = lax.broadcasted_iota(jnp.int32, (LANE, c_len), 0)
    carry_c = cc_ref[0]
    carry_r = cr_ref[0]
    for c in range(nc):
        z = sm_ref[0, c * c_len:(c + 1) * c_len, :]
        y = _gate_values(z, prow_ref[0:1, :], prow_ref[1:2, :], lane, ah, bh) if apply else z
        cs = jnp.dot(tril, y, precision=HIGHEST, preferred_element_type=F32)
        col_ref[0, c * c_len:(c + 1) * c_len, :] = _gate_merge(lane, cs, y, carry_c, ah, bh)
        carry_c = carry_c + cs[c_len - 1:c_len, :]

        zt = smt_ref[0, :, c * c_len:(c + 1) * c_len]
        yt = _gate_values(zt, pcol_ref[:, 0:1], pcol_ref[:, 1:2], subl, ah, bh) if apply else zt
        cst = jnp.dot(yt, triu, precision=HIGHEST, preferred_element_type=F32)
        out_t = _gate_merge(subl, cst, yt, carry_r, ah, bh)
        rowc_ref[0, c] = out_t
        rowf_ref[0, :, c * c_len:(c + 1) * c_len] = out_t
        carry_r = carry_r + cst[:, c_len - 1:c_len]


def _gates(sm3, smt, carry_col, carry_row, prow, pcol, *, apply, c_len, ah, bh):
    b, t_len, _ = sm3.shape
    nc = t_len // c_len
    return pl.pallas_call(
        functools.partial(_gates_kernel, apply, c_len, ah, bh),
        grid=(b,),
        in_specs=[
            pl.BlockSpec((1, t_len, LANE), lambda i: (i, 0, 0)),
            pl.BlockSpec((1, LANE, t_len), lambda i: (i, 0, 0)),
            pl.BlockSpec((1, 1, LANE), lambda i: (i, 0, 0)),
            pl.BlockSpec((1, LANE, 1), lambda i: (i, 0, 0)),
            pl.BlockSpec((2, LANE), lambda i: (0, 0)),
            pl.BlockSpec((LANE, 2), lambda i: (0, 0)),
        ],
        out_specs=[
            pl.BlockSpec((1, t_len, LANE), lambda i: (i, 0, 0)),
            pl.BlockSpec((1, nc, LANE, c_len), lambda i: (i, 0, 0, 0)),
            pl.BlockSpec((1, LANE, t_len), lambda i: (i, 0, 0)),
        ],
        out_shape=[
            jax.ShapeDtypeStruct((b, t_len, LANE), F32),
            jax.ShapeDtypeStruct((b, nc, LANE, c_len), F32),
            jax.ShapeDtypeStruct((b, LANE, t_len), F32),
        ],
        compiler_params=_params(("parallel",)),
        name="gates",
    )(sm3, smt, carry_col, carry_row, prow, pcol)


def _unit_lower_inverses(lows, eye, ii, jj):
    c_len = lows[0].shape[0]
    same0 = (ii >> 1) == (jj >> 1)
    ts = [eye - jnp.where(same0, low, 0.0) for low in lows]
    k = 1
    while (2 << k) <= c_len:
        sel = ((ii >> (k + 1)) == (jj >> (k + 1))) & (((ii >> k) & 1) == 1) & (((jj >> k) & 1) == 0)
        tbs = [t.astype(BF16) for t in ts]
        mid = [_dot(tb, jnp.where(sel, low, 0.0).astype(BF16)).astype(BF16) for tb, low in zip(tbs, lows)]
        ts = [t - _dot(m, tb) for t, m, tb in zip(ts, mid, tbs)]
        k += 1
    return ts


def _gdn_kernel(c_len, hb, ah, q_ref, k_ref, v_ref, z_ref, hq_ref, hk_ref, hv_ref,
                cq_ref, ck_ref, cv_ref, col_ref, rowc_ref, s0_ref, nw_ref,
                o_ref, sn_ref, s_scr, buf_scr):
    c_idx = pl.program_id(2)
    hd = nw_ref.shape[1]
    lo = CONV_PAD - 3

    @pl.when(c_idx == 0)
    def _():
        s_scr[...] = s0_ref[0]
        buf_scr[0, lo:CONV_PAD, :] = hq_ref[0]
        buf_scr[1, lo:CONV_PAD, :] = hk_ref[0]
        buf_scr[2, lo:CONV_PAD, :] = hv_ref[0]

    conv = []
    for n, (x_ref, w_ref) in enumerate(((q_ref, cq_ref), (k_ref, ck_ref), (v_ref, cv_ref))):
        buf_scr[n, CONV_PAD:CONV_PAD + c_len, :] = x_ref[0]
        acc = w_ref[0:1, :] * buf_scr[n, lo:lo + c_len, :]
        for w in range(1, 4):
            acc = acc + w_ref[w:w + 1, :] * buf_scr[n, lo + w:lo + w + c_len, :]
        buf_scr[n, lo:CONV_PAD, :] = buf_scr[n, lo + c_len:CONV_PAD + c_len, :]
        conv.append(_silu(acc))
    qc, kc, vc = conv

    colblk = col_ref[0]
    ii = lax.broadcasted_iota(jnp.int32, (c_len, c_len), 0)
    jj = lax.broadcasted_iota(jnp.int32, (c_len, c_len), 1)
    eye = (ii == jj).astype(F32)
    nw = nw_ref[...]
    heads = range(hb)
    sls = [slice(h * hd, (h + 1) * hd) for h in heads]

    cum_c = [colblk[:, h:h + 1] for h in heads]
    beta_c = [colblk[:, ah + h:ah + h + 1] for h in heads]
    cum_r = [rowc_ref[0, 0, h:h + 1, :] for h in heads]
    cum_last = [r[:, c_len - 1:c_len] for r in cum_r]
    e_cum = [jnp.exp(c) for c in cum_c]

    q = [qc[:, sl] for sl in sls]
    k = [kc[:, sl] for sl in sls]
    q = [x * (lax.rsqrt(jnp.sum(x * x, axis=1, keepdims=True) + L2_EPS) * (hd ** -0.5)) for x in q]
    k = [x * lax.rsqrt(jnp.sum(x * x, axis=1, keepdims=True) + L2_EPS) for x in k]
    qb = [x.astype(BF16) for x in q]
    kb = [x.astype(BF16) for x in k]

    decay = [jnp.exp(jnp.where(ii >= jj, c - r, NEG_INF)) for c, r in zip(cum_c, cum_r)]
    kk = [_dot_nt(x, x) for x in kb]
    qk = [_dot_nt(x, y) for x, y in zip(qb, kb)]
    lows = [jnp.where(ii > jj, b * m * d, 0.0) for b, m, d in zip(beta_c, kk, decay)]
    t_inv = _unit_lower_inverses(lows, eye, ii, jj)

    rhs = [jnp.concatenate([b * vc[:, sl], (b * e) * x], axis=1).astype(BF16)
           for b, e, x, sl in zip(beta_c, e_cum, k, sls)]
    w_all = [_dot(t.astype(BF16), r) for t, r in zip(t_inv, rhs)]

    s_old = [s_scr[h] for h in heads]
    sb = [s.astype(BF16) for s in s_old]
    u = [w[:, :hd] - _dot(w[:, hd:].astype(BF16), s) for w, s in zip(w_all, sb)]
    ub = [x.astype(BF16) for x in u]
    o = [_dot((x * e).astype(BF16), s) + _dot((m * d).astype(BF16), y)
         for x, e, s, m, d, y in zip(q, e_cum, sb, qk, decay, ub)]
    for h in heads:
        k_dec = k[h] * jnp.exp(cum_last[h] - cum_c[h])
        s_scr[h] = s_old[h] * jnp.exp(cum_last[h]) + _dot_tn(k_dec.astype(BF16), ub[h])
    for h in heads:
        x = o[h] * lax.rsqrt(jnp.mean(o[h] * o[h], axis=1, keepdims=True) + RMS_EPS) * nw
        o_ref[0, :, sls[h]] = (x * _silu(z_ref[0, :, sls[h]])).astype(o_ref.dtype)

    @pl.when(c_idx == pl.num_programs(2) - 1)
    def _():
        sn_ref[0] = s_scr[...]


def _gdn(proj3, conv_hist, conv_w, col, rowc, s0, norm_w, *, c_len, ah, hd):
    b, t_len, _ = proj3.shape
    nc = t_len // c_len
    aw = ah * hd
    hb = ah
    ng = ah // hb
    wb = hb * hd
    kq, kk_, kv, kz = 0, ng, 2 * ng, 3 * ng
    tok = lambda off: pl.BlockSpec((1, c_len, wb), lambda i, g, c: (i, c, off + g))
    hist = lambda off: pl.BlockSpec((1, 3, wb), lambda i, g, c: (i, 0, off + g))
    cw = lambda off: pl.BlockSpec((4, wb), lambda i, g, c: (0, off + g))
    return pl.pallas_call(
        functools.partial(_gdn_kernel, c_len, hb, ah),
        grid=(b, ng, nc),
        in_specs=[
            tok(kq), tok(kk_), tok(kv), tok(kz),
            hist(kq), hist(kk_), hist(kv),
            cw(kq), cw(kk_), cw(kv),
            pl.BlockSpec((1, c_len, LANE), lambda i, g, c: (i, c, 0)),
            pl.BlockSpec((1, 1, LANE, c_len), lambda i, g, c: (i, c, 0, 0)),
            pl.BlockSpec((1, hb, hd, hd), lambda i, g, c: (i, g, 0, 0)),
            pl.BlockSpec((1, hd), lambda i, g, c: (0, 0)),
        ],
        out_specs=[
            pl.BlockSpec((1, c_len, wb), lambda i, g, c: (i, c, g)),
            pl.BlockSpec((1, hb, hd, hd), lambda i, g, c: (i, g, 0, 0)),
        ],
        out_shape=[
            jax.ShapeDtypeStruct((b, t_len, aw), BF16),
            jax.ShapeDtypeStruct((b, ah, hd, hd), F32),
        ],
        scratch_shapes=[
            pltpu.VMEM((hb, hd, hd), F32),
            pltpu.VMEM((3, CONV_PAD + c_len, wb), F32),
        ],
        compiler_params=_params(("parallel", "parallel", "arbitrary")),
        name="gdn",
    )(proj3, proj3, proj3, proj3, conv_hist, conv_hist, conv_hist, conv_w, conv_w, conv_w,
      col, rowc, s0, norm_w)


def _fox_s_kernel(f_lane, bh, hd, q_ref, kn_ref, vn_ref, kc_ref, vc_ref, cq_ref, ckc_ref, ckn_ref, o_ref):
    t_q = q_ref.shape[1]
    p_len = kc_ref.shape[1] // bh
    heads = range(bh)
    sls = [slice(h * hd, (h + 1) * hd) for h in heads]
    q_scale = (hd ** -0.5) * LOG2E
    cqb = cq_ref[0] * LOG2E
    keep = (lax.broadcasted_iota(jnp.int32, (t_q, t_q), 1) <= lax.broadcasted_iota(jnp.int32, (t_q, t_q), 0))
    cq = [cqb[:, f_lane + h:f_lane + h + 1] for h in heads]
    ckc = [ckc_ref[0, f_lane + h:f_lane + h + 1, :] * LOG2E for h in heads]
    ckn = [ckn_ref[0, f_lane + h:f_lane + h + 1, :] * LOG2E for h in heads]
    qb = [(q_ref[0, :, sl] * q_scale).astype(BF16) for sl in sls]
    kc = [kc_ref[0, pl.ds(h, p_len, stride=bh), :].astype(BF16) for h in heads]
    xc = [_dot_nt(a, b) - c for a, b, c in zip(qb, kc, ckc)]
    xn = [jnp.where(keep, _dot_nt(a, kn_ref[0, :, sl].astype(BF16)) - c, NEG_INF)
          for a, sl, c in zip(qb, sls, ckn)]
    m = [c + jnp.maximum(jnp.max(a, axis=1, keepdims=True), jnp.max(b, axis=1, keepdims=True))
         for a, b, c in zip(xc, xn, cq)]
    pc = [jnp.exp2(a + (c - mm)) for a, c, mm in zip(xc, cq, m)]
    pn = [jnp.exp2(a + (c - mm)) for a, c, mm in zip(xn, cq, m)]
    vc = [vc_ref[0, pl.ds(h, p_len, stride=bh), :].astype(BF16) for h in heads]
    acc = [_dot(a.astype(BF16), b) + _dot(c.astype(BF16), vn_ref[0, :, sl].astype(BF16))
           for a, b, c, sl in zip(pc, vc, pn, sls)]
    for h in heads:
        den = jnp.sum(pc[h], axis=1, keepdims=True) + jnp.sum(pn[h], axis=1, keepdims=True)
        o_ref[0, :, sls[h]] = (acc[h] / den).astype(o_ref.dtype)


def _fox_s(q_arr, q_blk, k_new, v_new, k_cache, v_cache, cq_col, ck_cache_row, ck_new_row, *, bh, hd, f_lane):
    b, t_q, bw = k_new.shape
    rows = k_cache.shape[1]
    p_len = rows // bh
    return pl.pallas_call(
        functools.partial(_fox_s_kernel, f_lane, bh, hd),
        grid=(b,),
        in_specs=[
            pl.BlockSpec((1, t_q, bw), lambda i: (i, 0, q_blk)),
            pl.BlockSpec((1, t_q, bw), lambda i: (i, 0, 0)),
            pl.BlockSpec((1, t_q, bw), lambda i: (i, 0, 0)),
            pl.BlockSpec((1, rows, hd), lambda i: (i, 0, 0)),
            pl.BlockSpec((1, rows, hd), lambda i: (i, 0, 0)),
            pl.BlockSpec((1, t_q, LANE), lambda i: (i, 0, 0)),
            pl.BlockSpec((1, LANE, p_len), lambda i: (i, 0, 0)),
            pl.BlockSpec((1, LANE, t_q), lambda i: (i, 0, 0)),
        ],
        out_specs=pl.BlockSpec((1, t_q, bw), lambda i: (i, 0, 0)),
        out_shape=jax.ShapeDtypeStruct((b, t_q, bw), BF16),
        compiler_params=_params(("parallel",)),
        name="fox_s",
    )(q_arr, k_new, v_new, k_cache, v_cache, cq_col, ck_cache_row, ck_new_row)


def _fox_t_kernel(f_lane, hg, hd, k_ref, qt_ref, vt_ref, cq_ref, ck_ref, o_ref, m_scr, l_scr, acc_scr):
    g = pl.program_id(1)
    qi = pl.program_id(2)
    kj = pl.program_id(3)
    tk = k_ref.shape[1]
    tq = qt_ref.shape[1]
    heads = range(hg)
    sls = [slice(h * hd, (h + 1) * hd) for h in heads]
    k_scale = (hd ** -0.5) * LOG2E

    @pl.when(kj == 0)
    def _():
        m_scr[...] = jnp.full(m_scr.shape, NEG_INF, F32)
        l_scr[...] = jnp.zeros(l_scr.shape, F32)
        acc_scr[...] = jnp.zeros(acc_scr.shape, F32)

    def update(masked):
        ckb = ck_ref[0] * LOG2E
        lane = lax.broadcasted_iota(jnp.int32, ckb.shape, 1)
        ck = [jnp.sum(jnp.where(lane == f_lane + g * hg + h, ckb, 0.0), axis=1, keepdims=True)
              for h in heads]
        cq = [cq_ref[0, pl.ds(f_lane + g * hg + h, 1), :] * LOG2E for h in heads]
        kb = [(k_ref[0, :, sl] * k_scale).astype(BF16) for sl in sls]
        x = [_dot(a, qt_ref[sl, :]) - c for a, sl, c in zip(kb, sls, ck)]
        if masked:
            kpos = kj * tk + lax.broadcasted_iota(jnp.int32, (tk, tq), 0)
            qpos = qi * tq + lax.broadcasted_iota(jnp.int32, (tk, tq), 1)
            keep = kpos <= qpos
            x = [jnp.where(keep, a, NEG_INF) for a in x]
        m_old = [m_scr[h] for h in heads]
        m_new = [jnp.maximum(mo, c + jnp.max(a, axis=0, keepdims=True)) for mo, c, a in zip(m_old, cq, x)]
        p = [jnp.exp2(a + (c - mn)) for a, c, mn in zip(x, cq, m_new)]
        alpha = [jnp.exp2(mo - mn) for mo, mn in zip(m_old, m_new)]
        pv = [_dot(vt_ref[sl, :], a.astype(BF16)) for a, sl in zip(p, sls)]
        for h in heads:
            l_scr[h] = alpha[h] * l_scr[h] + jnp.sum(p[h], axis=0, keepdims=True)
            acc_scr[h] = alpha[h] * acc_scr[h] + pv[h]
            m_scr[h] = m_new[h]

    active = kj * tk <= qi * tq + (tq - 1)
    crosses = kj * tk + (tk - 1) > qi * tq

    @pl.when(active & crosses)
    def _():
        update(True)

    @pl.when(active & jnp.logical_not(crosses))
    def _():
        update(False)

    @pl.when(kj == pl.num_programs(3) - 1)
    def _():
        for h in heads:
            o_ref[0, :, sls[h]] = (acc_scr[h] / l_scr[h]).T.astype(o_ref.dtype)


def _fox_t(k3, qv_t, cq_row, ck_col, *, bh, hd, f_lane):
    b, t_len, bw = k3.shape
    tq = _tile(t_len, 512)
    tk = tq
    nq = t_len // tq
    hg = 16 if bh % 16 == 0 else bh
    wg = hg * hd
    ng = bh // hg
    last = lambda qi: (qi * tq + (tq - 1)) // tk
    kblk = lambda qi, kj: jnp.minimum(kj, last(qi))
    return pl.pallas_call(
        functools.partial(_fox_t_kernel, f_lane, hg, hd),
        grid=(b, ng, nq, nq),
        in_specs=[
            pl.BlockSpec((1, tk, wg), lambda i, g, qi, kj: (i, kblk(qi, kj), g)),
            pl.BlockSpec((wg, tq), lambda i, g, qi, kj: (g, i * nq + qi)),
            pl.BlockSpec((wg, tk), lambda i, g, qi, kj: (ng + g, i * nq + kblk(qi, kj))),
            pl.BlockSpec((1, LANE, tq), lambda i, g, qi, kj: (i, 0, qi)),
            pl.BlockSpec((1, tk, LANE), lambda i, g, qi, kj: (i, kblk(qi, kj), 0)),
        ],
        out_specs=pl.BlockSpec((1, tq, wg), lambda i, g, qi, kj: (i, qi, g)),
        out_shape=jax.ShapeDtypeStruct((b, t_len, bw), BF16),
        scratch_shapes=[
            pltpu.VMEM((hg, 1, tq), F32),
            pltpu.VMEM((hg, 1, tq), F32),
            pltpu.VMEM((hg, hd, tq), F32),
        ],
        compiler_params=_params(("parallel", "parallel", "parallel", "arbitrary")),
        name="fox_t",
    )(k3, qv_t, qv_t, cq_row, ck_col)


def _layer_norm_rows(x, g, b):
    mu = jnp.mean(x, axis=1, keepdims=True)
    xc = x - mu
    var = jnp.mean(xc * xc, axis=1, keepdims=True)
    return xc * lax.rsqrt(var + LN_EPS) * g + b


def _layer_norm_ref(ref, g_ref, b_ref, rows):
    def body(r, carry):
        sl = pl.ds(pl.multiple_of(r * rows, rows), rows)
        ref[sl, :] = _layer_norm_rows(ref[sl, :], g_ref[...], b_ref[...])
        return carry
    lax.fori_loop(0, ref.shape[0] // rows, body, 0)


def _outproj_kernel(alpha, tn, oa_ref, ob_ref, wa_ref, wb_ref, x_ref, g_ref, b_ref, hid_ref, hidt_ref):
    j = pl.program_id(1)
    col = pl.multiple_of(j * tn, LANE)
    hid_ref[:, pl.ds(col, tn)] = (alpha * x_ref[...] + _dot(oa_ref[...], wa_ref[...])
                                  + _dot(ob_ref[...], wb_ref[...]))

    @pl.when(j == pl.num_programs(1) - 1)
    def _():
        tm, d = hid_ref.shape
        _layer_norm_ref(hid_ref, g_ref, b_ref, min(tm, 32))
        rb = min(tm, LANE)
        step = _tile(d, 512)
        for r in range(tm // rb):
            for c in range(d // step):
                hidt_ref[c * step:(c + 1) * step, r * rb:(r + 1) * rb] = (
                    hid_ref[r * rb:(r + 1) * rb, c * step:(c + 1) * step].T.astype(BF16))


def _outproj(o_a, o_b, w_a, w_b, x2d, g, b, alpha):
    m, d = x2d.shape
    tm = _tile(m, 512)
    tn = _tile(d, 512)
    aw, bw = o_a.shape[1], o_b.shape[1]
    assert w_a.shape[0] == aw + bw and aw % bw == 0
    return pl.pallas_call(
        functools.partial(_outproj_kernel, alpha, tn),
        grid=(m // tm, d // tn),
        in_specs=[
            pl.BlockSpec((tm, o_a.shape[1]), lambda i, j: (i, 0)),
            pl.BlockSpec((tm, o_b.shape[1]), lambda i, j: (i, 0)),
            pl.BlockSpec((aw, tn), lambda i, j: (0, j)),
            pl.BlockSpec((bw, tn), lambda i, j: (aw // bw, j)),
            pl.BlockSpec((tm, tn), lambda i, j: (i, j)),
            pl.BlockSpec((1, d), lambda i, j: (0, 0)),
            pl.BlockSpec((1, d), lambda i, j: (0, 0)),
        ],
        out_specs=[
            pl.BlockSpec((tm, d), lambda i, j: (i, 0)),
            pl.BlockSpec((d, tm), lambda i, j: (0, i)),
        ],
        out_shape=[
            jax.ShapeDtypeStruct((m, d), F32),
            jax.ShapeDtypeStruct((d, m), BF16),
        ],
        compiler_params=_params(("parallel", "arbitrary")),
        name="outproj",
    )(o_a, o_b, w_a, w_b, x2d, g, b)


def _top_values(x, n):
    vals = []
    for _ in range(n):
        m = jnp.max(x, axis=0, keepdims=True)
        vals.append(m)
        x = jnp.where(x == m, NEG_INF, x)
    return vals


def _route_kernel(topk, hp, ht_ref, wq_ref, key_ref, a1_ref, a2_ref, thr_ref, cand_scr):
    dk = key_ref.shape[3]
    n = topk + 1
    pairs = [(a, b) for a in range(n) for b in range(n) if (a + 1) * (b + 1) <= n]

    def scores(h):
        qt = _dot(wq_ref[h * 2 * dk:(h + 1) * 2 * dk, :], ht_ref[...])
        return (_dot(key_ref[h, 0], qt[:dk].astype(BF16)),
                _dot(key_ref[h, 1], qt[dk:].astype(BF16)))

    nxt = scores(0)
    for h in range(hp):
        s1, s2 = nxt
        if h + 1 < hp:
            nxt = scores(h + 1)
        top1 = _top_values(s1, n)
        top2 = _top_values(s2, n)
        cand_scr[...] = jnp.full(cand_scr.shape, NEG_INF, F32)
        for r, (a, b) in enumerate(pairs):
            cand_scr[r:r + 1, :] = top1[a] + top2[b]
        best = _top_values(cand_scr[...], n)
        z = jnp.exp(best[0] - best[0])
        for t in best[1:topk]:
            z = z + jnp.exp(t - best[0])
        a1_ref[h] = jnp.exp(s1 - top1[0]) / z
        a2_ref[h] = jnp.exp(s2 - top2[0])
        thr = jnp.exp(0.5 * (best[topk - 1] + best[topk]) - best[0]) / z
        thr_ref[h] = jnp.broadcast_to(thr, thr_ref.shape[1:])


def _num_candidates(n):
    return -(-sum(n // (a + 1) for a in range(n)) // 8) * 8


def _route(hid_t, wq_t, keys, topk):
    d, m = hid_t.shape
    ph, _, nk, dk = keys.shape
    tm = _tile(m, 512)
    hp = 8 if ph % 8 == 0 else ph
    out = jax.ShapeDtypeStruct((ph, nk, m), F32)
    blk = pl.BlockSpec((hp, nk, tm), lambda i, h: (h, 0, i))
    return pl.pallas_call(
        functools.partial(_route_kernel, topk, hp),
        grid=(m // tm, ph // hp),
        in_specs=[
            pl.BlockSpec((d, tm), lambda i, h: (0, i)),
            pl.BlockSpec((hp * 2 * dk, d), lambda i, h: (h, 0)),
            pl.BlockSpec((hp, 2, nk, dk), lambda i, h: (h, 0, 0, 0)),
        ],
        out_specs=[blk, blk, pl.BlockSpec((hp, 8, tm), lambda i, h: (h, 0, i))],
        out_shape=[out, out, jax.ShapeDtypeStruct((ph, 8, m), F32)],
        scratch_shapes=[pltpu.VMEM((_num_candidates(topk + 1), tm), F32)],
        compiler_params=_params(("parallel", "arbitrary")),
        name="peer_route",
    )(hid_t, wq_t, keys)


def _peer_kernel(ht_ref, u_ref, vt_ref, a1_ref, a2_ref, thr_ref, o_ref, act_scr, pre_scr):
    e = pl.program_id(1)
    n_tiles = pl.num_programs(1) - 1
    ph, nk, tm = a2_ref.shape
    te = u_ref.shape[0]
    n_sub = te // nk
    d = o_ref.shape[0]
    slot = e % 2
    strip = min(tm, LANE)

    @pl.when(e == 0)
    def _():
        o_ref[...] = jnp.zeros(o_ref.shape, F32)
        act_scr[1] = jnp.zeros(act_scr.shape[1:], BF16)

    @pl.when(e < n_tiles)
    def _():
        pre_scr[...] = _dot(u_ref[...], ht_ref[...])

    n_ch = PEER_DRAIN_TRIPS
    step = d // n_ch
    gr = te // n_ch
    for c in range(n_ch):
        sl = slice(c * step, (c + 1) * step)
        o_ref[sl, :] += _dot(vt_ref[sl, :], act_scr[1 - slot])
        for r0 in range(c * gr, (c + 1) * gr, min(gr, nk)):
            rn = min(gr, nk)
            row = jnp.minimum(e * n_sub + r0 // nk, nk - 1)
            j0 = r0 % nk
            a1_rows = [a1_ref[h, pl.ds(row, 1), :] for h in range(ph)]
            for t in range(tm // strip):
                ls = slice(t * strip, (t + 1) * strip)
                gate = None
                for h in range(ph):
                    g = a2_ref[h, j0:j0 + rn, ls] * a1_rows[h][:, ls]
                    term = jnp.where(g >= thr_ref[h, 0:1, ls], g, 0.0)
                    gate = term if gate is None else gate + term
                x = pre_scr[r0:r0 + rn, ls]
                act = gate * (0.5 * x * (1.0 + lax.erf(x * (2.0 ** -0.5))))
                act_scr[slot, r0:r0 + rn, ls] = act.astype(BF16)


def _peer(hid_t, u_b, v_t, a1, a2, thr):
    d, m = hid_t.shape
    ne = u_b.shape[0]
    ph, nk, _ = a1.shape
    tm = _tile(m, 512)
    te = _tile(ne, 512)
    n_tiles = ne // te
    assert te % nk == 0 and d % (te // nk) == 0
    rt = pl.BlockSpec((ph, nk, tm), lambda i, e: (0, 0, i))
    return pl.pallas_call(
        _peer_kernel,
        grid=(m // tm, n_tiles + 1),
        in_specs=[
            pl.BlockSpec((d, tm), lambda i, e: (0, i)),
            pl.BlockSpec((te, d), lambda i, e: (jnp.minimum(e, n_tiles - 1), 0)),
            pl.BlockSpec((d, te), lambda i, e: (0, jnp.maximum(e - 1, 0))),
            rt, rt,
            pl.BlockSpec((ph, 8, tm), lambda i, e: (0, 0, i)),
        ],
        out_specs=pl.BlockSpec((d, tm), lambda i, e: (0, i)),
        out_shape=jax.ShapeDtypeStruct((d, m), F32),
        scratch_shapes=[pltpu.VMEM((2, te, tm), BF16), pltpu.VMEM((te, tm), F32)],
        compiler_params=_params(("parallel", "arbitrary")),
        name="peer_dense",
    )(hid_t, u_b, v_t, a1, a2, thr)


def _final_kernel(alpha, hid_ref, pt_ref, g_ref, b_ref, y_ref):
    d = hid_ref.shape[1]
    step = _tile(d, 512)
    for c in range(d // step):
        sl = slice(c * step, (c + 1) * step)
        y_ref[:, sl] = alpha * hid_ref[:, sl] + pt_ref[sl, :].T
    _layer_norm_ref(y_ref, g_ref, b_ref, min(y_ref.shape[0], 32))


def _final(hid, peer_t, g, b, alpha):
    m, d = hid.shape
    tm = _tile(m, 256)
    return pl.pallas_call(
        functools.partial(_final_kernel, alpha),
        grid=(m // tm,),
        in_specs=[
            pl.BlockSpec((tm, d), lambda i: (i, 0)),
            pl.BlockSpec((d, tm), lambda i: (0, i)),
            pl.BlockSpec((1, d), lambda i: (0, 0)),
            pl.BlockSpec((1, d), lambda i: (0, 0)),
        ],
        out_specs=pl.BlockSpec((tm, d), lambda i: (i, 0)),
        out_shape=jax.ShapeDtypeStruct((m, d), F32),
        compiler_params=_params(("parallel",)),
        name="final_ln",
    )(hid, peer_t, g, b)


def _pad_lanes(a):
    return jnp.pad(a, [(0, 0)] * (a.ndim - 1) + [(0, LANE - a.shape[-1])])


def _trunk(x, conv_hist, s0, fox_cache, wts, depth):
    (w_main, w_small, conv_w, prow, pcol, norm_w, w_a, w_b, ln1_g, ln1_b, wq_t, keys, u_b, v_t,
     ln2_g, ln2_b, ah, bh, hd, topk) = wts
    b, t_len, d = x.shape
    m = b * t_len
    aw, bw = ah * hd, bh * hd
    alpha = (2 * depth) ** 0.25
    f_lane = 2 * ah
    x2d = x.reshape(m, d)

    n_a = 4 * aw
    prompt = fox_cache is None
    proj, k_new, v_new, sm, smt, *qv_t = _inproj(x2d, w_main, w_small, n_a, bw, prompt)
    proj3 = proj.reshape(b, t_len, -1)
    k_new = k_new.reshape(b, t_len, bw)
    v_new = v_new.reshape(b, t_len, bw)

    c_len = min(64, t_len)
    zero_c = jnp.zeros((b, 1, LANE), F32)
    zero_r = jnp.zeros((b, LANE, 1), F32)
    if fox_cache is None:
        carry_c, carry_r = zero_c, zero_r
    else:
        clf = fox_cache[2].astype(F32)
        p_len = clf.shape[1]
        clf_col = jnp.pad(clf, ((0, 0), (0, 0), (f_lane, LANE - f_lane - bh)))
        clf_row = jnp.swapaxes(clf_col, 1, 2)
        cc_col, _, cc_row = _gates(clf_col, clf_row, zero_c, zero_r, prow, pcol,
                                   apply=False, c_len=min(64, p_len), ah=ah, bh=bh)
        carry_c = cc_col[:, p_len - 1:, :]
        carry_r = cc_row[:, :, p_len - 1:]
    smt3 = jnp.swapaxes(smt.reshape(LANE, b, t_len), 0, 1)
    col, rowc, rowf = _gates(sm.reshape(b, t_len, LANE), smt3, carry_c, carry_r, prow, pcol,
                             apply=True, c_len=c_len, ah=ah, bh=bh)
    logf = col[:, :, f_lane + bh:f_lane + 2 * bh]

    o_a, s_new = _gdn(proj3, conv_hist, conv_w, col, rowc, s0, norm_w, c_len=c_len, ah=ah, hd=hd)
    conv_new = jnp.concatenate([conv_hist.astype(F32), proj3[:, :, :3 * aw]], axis=1)[:, -3:]

    qb0 = n_a // hd
    if prompt:
        o_b = _fox_t(k_new, qv_t[0], rowf, col, bh=bh, hd=hd, f_lane=f_lane)
    else:
        assert n_a % bw == 0
        k_cache = fox_cache[0].reshape(b, -1, hd).astype(F32)
        v_cache = fox_cache[1].reshape(b, -1, hd).astype(F32)
        o_b = _fox_s(proj3, n_a // bw, k_new, v_new, k_cache, v_cache, col, cc_row, rowf,
                     bh=bh, hd=hd, f_lane=f_lane)

    hid, hid_t = _outproj(o_a.reshape(m, aw), o_b.reshape(m, bw), w_a, w_b, x2d, ln1_g, ln1_b, alpha)
    a1, a2, thr = _route(hid_t, wq_t, keys, topk)
    peer_t = _peer(hid_t, u_b, v_t, a1, a2, thr)
    y = _final(hid, peer_t, ln2_g, ln2_b, alpha).reshape(b, t_len, d)
    return y, (k_new.reshape(b, t_len, bh, hd), v_new.reshape(b, t_len, bh, hd), logf, s_new, conv_new)


def kernel(x_prompt, x_sample, cache_fox_k, cache_fox_v, cache_fox_logf, state_gdn, state_gdn_conv,
           w_in, gdn_conv_w, gdn_a_log, gdn_dt_bias, gdn_norm_w, fox_f_bias, w_out, ln1_g, ln1_b,
           peer_w_q, peer_sub_keys, peer_u, peer_v, ln2_g, ln2_b):
    depth = w_in.shape[0]
    ah = gdn_a_log.shape[1]
    bh = fox_f_bias.shape[1]
    hd = gdn_norm_w.shape[1]
    aw, bw = ah * hd, bh * hd
    topk = 16
    assert 2 * ah + 2 * bh <= LANE
    n_p = x_prompt.shape[0]
    yp, ys = x_prompt, x_sample
    outs_p, outs_s = [], []
    for l in range(depth):
        o_a_a = 4 * aw
        o_b_qkv = o_a_a + 2 * ah
        o_b_f = o_b_qkv + 3 * bw
        wl = w_in[l]
        w_main = jnp.concatenate([wl[:, :o_a_a].astype(BF16), wl[:, o_b_qkv:o_b_f].astype(BF16)], axis=1)
        w_out_b = w_out[l].astype(BF16)
        w_f = wl[:, o_b_f:o_b_f + bh]
        w_small = _pad_lanes(jnp.concatenate([wl[:, o_a_a:o_b_qkv], w_f, w_f], axis=1)).astype(BF16)
        zeros_a = jnp.zeros((ah,), F32)
        prow = jnp.stack([
            _pad_lanes(gdn_a_log[l].astype(F32)),
            _pad_lanes(jnp.concatenate([gdn_dt_bias[l].astype(F32), zeros_a,
                                        fox_f_bias[l].astype(F32), fox_f_bias[l].astype(F32)])),
        ])
        wts = (w_main, w_small, gdn_conv_w[l].astype(F32), prow, prow.T,
               gdn_norm_w[l].reshape(1, hd).astype(F32),
               w_out_b, w_out_b,
               ln1_g[l].reshape(1, -1), ln1_b[l].reshape(1, -1),
               peer_w_q[l].T.astype(BF16), peer_sub_keys[l].astype(BF16),
               peer_u[l].astype(BF16), peer_v[l].T.astype(BF16),
               ln2_g[l].reshape(1, -1), ln2_b[l].reshape(1, -1), ah, bh, hd, topk)
        conv0 = jnp.zeros((n_p, 3, 3 * aw), yp.dtype)
        s0 = jnp.zeros((n_p, ah, hd, hd), yp.dtype)
        yp, st_p = _trunk(yp, conv0, s0, None, wts, depth)
        ys, st_s = _trunk(ys, state_gdn_conv[l], state_gdn[l],
                          (cache_fox_k[l], cache_fox_v[l], cache_fox_logf[l]), wts, depth)
        outs_p.append(st_p)
        outs_s.append(st_s)
    stack = lambda outs, n: jnp.stack([o[n] for o in outs], axis=0)
    return ((yp, ys) + tuple(stack(outs_p, n) for n in range(5))
            + tuple(stack(outs_s, n) for n in range(5)))
```

```python
import functools

import jax
import jax.numpy as jnp
from jax import lax
from jax.experimental import pallas as pl
from jax.experimental.pallas import tpu as pltpu

F32 = jnp.float32
BF16 = jnp.bfloat16
LANE = 128
LN_EPS = 1e-5
RMS_EPS = 1e-6
L2_EPS = 1e-6
CONV_PAD = 8
VMEM_LIMIT = 58 * 1024 * 1024
HIGHEST = lax.Precision.HIGHEST
NEG_INF = float("-inf")
LOG2E = 1.4426950408889634
PEER_DRAIN_TRIPS = 16


def _params(sem):
    return pltpu.CompilerParams(dimension_semantics=sem, vmem_limit_bytes=VMEM_LIMIT)


def _tile(n, pref, align=LANE):
    if n <= pref:
        return n
    t = (pref // align) * align
    while t >= align:
        if n % t == 0:
            return t
        t -= align
    return n


def _dot(a, b):
    return jnp.dot(a, b, preferred_element_type=F32)


def _dot_nt(a, b):
    return lax.dot_general(a, b, (((1,), (1,)), ((), ())), preferred_element_type=F32)


def _dot_tn(a, b):
    return lax.dot_general(a, b, (((0,), (0,)), ((), ())), preferred_element_type=F32)


def _sigmoid(x):
    return 1.0 / (1.0 + jnp.exp(-x))


def _softplus(x):
    return jnp.maximum(x, 0.0) + jnp.log1p(jnp.exp(-jnp.abs(x)))


def _silu(x):
    return x * _sigmoid(x)


def _inproj_kernel(j_q, j_k, j_v, emit_t, x_ref, w_ref, ws_ref, o_ref, k_ref, v_ref, s_ref, st_ref, *rest):
    xb_ref = rest[-1]
    j = pl.program_id(1)

    @pl.when(j == 0)
    def _():
        xb = x_ref[...].astype(BF16)
        xb_ref[...] = xb
        sm = _dot(xb, ws_ref[...])
        s_ref[...] = sm
        st_ref[...] = sm.T

    @pl.when(j < j_q)
    def _():
        o_ref[...] = _dot(xb_ref[...], w_ref[...])

    @pl.when((j >= j_q) & (j < j_k))
    def _():
        acc = _dot(xb_ref[...], w_ref[...])
        o_ref[...] = acc
        if emit_t:
            rest[0][...] = acc.T.astype(BF16)

    @pl.when((j >= j_k) & (j < j_v))
    def _():
        k_ref[...] = _dot(xb_ref[...], w_ref[...])

    @pl.when(j >= j_v)
    def _():
        acc = _dot(xb_ref[...], w_ref[...])
        v_ref[...] = acc
        if emit_t:
            rest[0][...] = acc.T.astype(BF16)


def _inproj(x2d, w_main, w_small, n_a, bw, emit_t):
    m, d = x2d.shape
    n = w_main.shape[1]
    assert n == n_a + 3 * bw
    tm = _tile(m, 512)
    tn = _tile(bw, 1024)
    assert n_a % tn == 0 and bw % tn == 0
    nb = bw // tn
    j_q = n_a // tn
    j_k = j_q + nb
    j_v = j_k + nb
    out_specs = [
        pl.BlockSpec((tm, tn), lambda i, j: (i, jnp.minimum(j, j_k - 1))),
        pl.BlockSpec((tm, tn), lambda i, j: (i, jnp.clip(j - j_k, 0, nb - 1))),
        pl.BlockSpec((tm, tn), lambda i, j: (i, jnp.maximum(j - j_v, 0))),
        pl.BlockSpec((tm, LANE), lambda i, j: (i, 0)),
        pl.BlockSpec((LANE, tm), lambda i, j: (0, i)),
    ]
    out_shape = [
        jax.ShapeDtypeStruct((m, n_a + bw), F32),
        jax.ShapeDtypeStruct((m, bw), F32),
        jax.ShapeDtypeStruct((m, bw), F32),
        jax.ShapeDtypeStruct((m, LANE), F32),
        jax.ShapeDtypeStruct((LANE, m), F32),
    ]
    if emit_t:
        row_blk = lambda j: jnp.where(j < j_k, jnp.maximum(j - j_q, 0),
                                      jnp.where(j < j_v, nb - 1, j - j_v + nb))
        out_specs.append(pl.BlockSpec((tn, tm), lambda i, j: (row_blk(j), i)))
        out_shape.append(jax.ShapeDtypeStruct((2 * bw, m), BF16))
    return pl.pallas_call(
        functools.partial(_inproj_kernel, j_q, j_k, j_v, emit_t),
        grid=(m // tm, n // tn),
        in_specs=[
            pl.BlockSpec((tm, d), lambda i, j: (i, 0)),
            pl.BlockSpec((d, tn), lambda i, j: (0, j)),
            pl.BlockSpec((d, LANE), lambda i, j: (0, 0)),
        ],
        out_specs=out_specs,
        out_shape=out_shape,
        scratch_shapes=[pltpu.VMEM((tm, d), BF16)],
        compiler_params=_params(("parallel", "arbitrary")),
        name="inproj",
    )(x2d, w_main, w_small)


def _gate_values(z, a_log, bias, idx, ah, bh):
    zz = z + bias
    g = -jnp.exp(a_log) * _softplus(zz)
    beta = _sigmoid(zz)
    logf = -_softplus(-zz)
    return jnp.where(idx < ah, g, jnp.where(idx < 2 * ah, beta, jnp.where(idx < 2 * ah + 2 * bh, logf, 0.0)))


def _gate_merge(idx, cs, y, carry, ah, bh):
    return jnp.where(idx < ah, cs,
                     jnp.where(idx < 2 * ah, y,
                               jnp.where(idx < 2 * ah + bh, cs + carry,
                                         jnp.where(idx < 2 * ah + 2 * bh, y, 0.0))))


def _gates_kernel(apply, c_len, ah, bh, sm_ref, smt_ref, cc_ref, cr_ref, prow_ref, pcol_ref,
                  col_ref, rowc_ref, rowf_ref):
    t_len = sm_ref.shape[1]
    nc = t_len // c_len
    ii = lax.broadcasted_iota(jnp.int32, (c_len, c_len), 0)
    jj = lax.broadcasted_iota(jnp.int32, (c_len, c_len), 1)
    tril = (ii >= jj).astype(F32)
    triu = (ii <= jj).astype(F32)
    lane = lax.broadcasted_iota(jnp.int32, (c_len, LANE), 1)
    subl = lax.broadcasted_iota(jnp.int32, (LANE, c_len), 0)
    carry_c = cc_ref[0]
    carry_r = cr_ref[0]
    for c in range(nc):
        z = sm_ref[0, c * c_len:(c + 1) * c_len, :]
        y = _gate_values(z, prow_ref[0:1, :], prow_ref[1:2, :], lane, ah, bh) if apply else z
        cs = jnp.dot(tril, y, precision=HIGHEST, preferred_element_type=F32)
        col_ref[0, c * c_len:(c + 1) * c_len, :] = _gate_merge(lane, cs, y, carry_c, ah, bh)
        carry_c = carry_c + cs[c_len - 1:c_len, :]

        zt = smt_ref[0, :, c * c_len:(c + 1) * c_len]
        yt = _gate_values(zt, pcol_ref[:, 0:1], pcol_ref[:, 1:2], subl, ah, bh) if apply else zt
        cst = jnp.dot(yt, triu, precision=HIGHEST, preferred_element_type=F32)
        out_t = _gate_merge(subl, cst, yt, carry_r, ah, bh)
        rowc_ref[0, c] = out_t
        rowf_ref[0, :, c * c_len:(c + 1) * c_len] = out_t
        carry_r = carry_r + cst[:, c_len - 1:c_len]


def _gates(sm3, smt, carry_col, carry_row, prow, pcol, *, apply, c_len, ah, bh):
    b, t_len, _ = sm3.shape
    nc = t_len // c_len
    return pl.pallas_call(
        functools.partial(_gates_kernel, apply, c_len, ah, bh),
        grid=(b,),
        in_specs=[
            pl.BlockSpec((1, t_len, LANE), lambda i: (i, 0, 0)),
            pl.BlockSpec((1, LANE, t_len), lambda i: (i, 0, 0)),
            pl.BlockSpec((1, 1, LANE), lambda i: (i, 0, 0)),
            pl.BlockSpec((1, LANE, 1), lambda i: (i, 0, 0)),
            pl.BlockSpec((2, LANE), lambda i: (0, 0)),
            pl.BlockSpec((LANE, 2), lambda i: (0, 0)),
        ],
        out_specs=[
            pl.BlockSpec((1, t_len, LANE), lambda i: (i, 0, 0)),
            pl.BlockSpec((1, nc, LANE, c_len), lambda i: (i, 0, 0, 0)),
            pl.BlockSpec((1, LANE, t_len), lambda i: (i, 0, 0)),
        ],
        out_shape=[
            jax.ShapeDtypeStruct((b, t_len, LANE), F32),
            jax.ShapeDtypeStruct((b, nc, LANE, c_len), F32),
            jax.ShapeDtypeStruct((b, LANE, t_len), F32),
        ],
        compiler_params=_params(("parallel",)),
        name="gates",
    )(sm3, smt, carry_col, carry_row, prow, pcol)


def _unit_lower_inverses(lows, eye, ii, jj):
    c_len = lows[0].shape[0]
    same0 = (ii >> 1) == (jj >> 1)
    ts = [eye - jnp.where(same0, low, 0.0) for low in lows]
    k = 1
    while (2 << k) <= c_len:
        sel = ((ii >> (k + 1)) == (jj >> (k + 1))) & (((ii >> k) & 1) == 1) & (((jj >> k) & 1) == 0)
        tbs = [t.astype(BF16) for t in ts]
        mid = [_dot(tb, jnp.where(sel, low, 0.0).astype(BF16)).astype(BF16) for tb, low in zip(tbs, lows)]
        ts = [t - _dot(m, tb) for t, m, tb in zip(ts, mid, tbs)]
        k += 1
    return ts


def _gdn_kernel(c_len, hb, ah, q_ref, k_ref, v_ref, z_ref, hq_ref, hk_ref, hv_ref,
                cq_ref, ck_ref, cv_ref, col_ref, rowc_ref, s0_ref, nw_ref,
                o_ref, sn_ref, s_scr, buf_scr):
    c_idx = pl.program_id(2)
    hd = nw_ref.shape[1]
    lo = CONV_PAD - 3

    @pl.when(c_idx == 0)
    def _():
        s_scr[...] = s0_ref[0]
        buf_scr[0, lo:CONV_PAD, :] = hq_ref[0]
        buf_scr[1, lo:CONV_PAD, :] = hk_ref[0]
        buf_scr[2, lo:CONV_PAD, :] = hv_ref[0]

    conv = []
    for n, (x_ref, w_ref) in enumerate(((q_ref, cq_ref), (k_ref, ck_ref), (v_ref, cv_ref))):
        buf_scr[n, CONV_PAD:CONV_PAD + c_len, :] = x_ref[0]
        acc = w_ref[0:1, :] * buf_scr[n, lo:lo + c_len, :]
        for w in range(1, 4):
            acc = acc + w_ref[w:w + 1, :] * buf_scr[n, lo + w:lo + w + c_len, :]
        buf_scr[n, lo:CONV_PAD, :] = buf_scr[n, lo + c_len:CONV_PAD + c_len, :]
        conv.append(_silu(acc))
    qc, kc, vc = conv

    colblk = col_ref[0]
    ii = lax.broadcasted_iota(jnp.int32, (c_len, c_len), 0)
    jj = lax.broadcasted_iota(jnp.int32, (c_len, c_len), 1)
    eye = (ii == jj).astype(F32)
    nw = nw_ref[...]
    heads = range(hb)
    sls = [slice(h * hd, (h + 1) * hd) for h in heads]

    cum_c = [colblk[:, h:h + 1] for h in heads]
    beta_c = [colblk[:, ah + h:ah + h + 1] for h in heads]
    cum_r = [rowc_ref[0, 0, h:h + 1, :] for h in heads]
    cum_last = [r[:, c_len - 1:c_len] for r in cum_r]
    e_cum = [jnp.exp(c) for c in cum_c]

    q = [qc[:, sl] for sl in sls]
    k = [kc[:, sl] for sl in sls]
    q = [x * (lax.rsqrt(jnp.sum(x * x, axis=1, keepdims=True) + L2_EPS) * (hd ** -0.5)) for x in q]
    k = [x * lax.rsqrt(jnp.sum(x * x, axis=1, keepdims=True) + L2_EPS) for x in k]
    qb = [x.astype(BF16) for x in q]
    kb = [x.astype(BF16) for x in k]

    decay = [jnp.exp(jnp.where(ii >= jj, c - r, NEG_INF)) for c, r in zip(cum_c, cum_r)]
    kk = [_dot_nt(x, x) for x in kb]
    qk = [_dot_nt(x, y) for x, y in zip(qb, kb)]
    lows = [jnp.where(ii > jj, b * m * d, 0.0) for b, m, d in zip(beta_c, kk, decay)]
    t_inv = _unit_lower_inverses(lows, eye, ii, jj)

    rhs = [jnp.concatenate([b * vc[:, sl], (b * e) * x], axis=1).astype(BF16)
           for b, e, x, sl in zip(beta_c, e_cum, k, sls)]
    w_all = [_dot(t.astype(BF16), r) for t, r in zip(t_inv, rhs)]

    s_old = [s_scr[h] for h in heads]
    sb = [s.astype(BF16) for s in s_old]
    u = [w[:, :hd] - _dot(w[:, hd:].astype(BF16), s) for w, s in zip(w_all, sb)]
    ub = [x.astype(BF16) for x in u]
    o = [_dot((x * e).astype(BF16), s) + _dot((m * d).astype(BF16), y)
         for x, e, s, m, d, y in zip(q, e_cum, sb, qk, decay, ub)]
    for h in heads:
        k_dec = k[h] * jnp.exp(cum_last[h] - cum_c[h])
        s_scr[h] = s_old[h] * jnp.exp(cum_last[h]) + _dot_tn(k_dec.astype(BF16), ub[h])
    for h in heads:
        x = o[h] * lax.rsqrt(jnp.mean(o[h] * o[h], axis=1, keepdims=True) + RMS_EPS) * nw
        o_ref[0, :, sls[h]] = (x * _silu(z_ref[0, :, sls[h]])).astype(o_ref.dtype)

    @pl.when(c_idx == pl.num_programs(2) - 1)
    def _():
        sn_ref[0] = s_scr[...]


def _gdn(proj3, conv_hist, conv_w, col, rowc, s0, norm_w, *, c_len, ah, hd):
    b, t_len, _ = proj3.shape
    nc = t_len // c_len
    aw = ah * hd
    hb = ah
    ng = ah // hb
    wb = hb * hd
    kq, kk_, kv, kz = 0, ng, 2 * ng, 3 * ng
    tok = lambda off: pl.BlockSpec((1, c_len, wb), lambda i, g, c: (i, c, off + g))
    hist = lambda off: pl.BlockSpec((1, 3, wb), lambda i, g, c: (i, 0, off + g))
    cw = lambda off: pl.BlockSpec((4, wb), lambda i, g, c: (0, off + g))
    return pl.pallas_call(
        functools.partial(_gdn_kernel, c_len, hb, ah),
        grid=(b, ng, nc),
        in_specs=[
            tok(kq), tok(kk_), tok(kv), tok(kz),
            hist(kq), hist(kk_), hist(kv),
            cw(kq), cw(kk_), cw(kv),
            pl.BlockSpec((1, c_len, LANE), lambda i, g, c: (i, c, 0)),
            pl.BlockSpec((1, 1, LANE, c_len), lambda i, g, c: (i, c, 0, 0)),
            pl.BlockSpec((1, hb, hd, hd), lambda i, g, c: (i, g, 0, 0)),
            pl.BlockSpec((1, hd), lambda i, g, c: (0, 0)),
        ],
        out_specs=[
            pl.BlockSpec((1, c_len, wb), lambda i, g, c: (i, c, g)),
            pl.BlockSpec((1, hb, hd, hd), lambda i, g, c: (i, g, 0, 0)),
        ],
        out_shape=[
            jax.ShapeDtypeStruct((b, t_len, aw), BF16),
            jax.ShapeDtypeStruct((b, ah, hd, hd), F32),
        ],
        scratch_shapes=[
            pltpu.VMEM((hb, hd, hd), F32),
            pltpu.VMEM((3, CONV_PAD + c_len, wb), F32),
        ],
        compiler_params=_params(("parallel", "parallel", "arbitrary")),
        name="gdn",
    )(proj3, proj3, proj3, proj3, conv_hist, conv_hist, conv_hist, conv_w, conv_w, conv_w,
      col, rowc, s0, norm_w)


def _fox_s_kernel(f_lane, bh, hd, q_ref, kn_ref, vn_ref, kc_ref, vc_ref, cq_ref, ckc_ref, ckn_ref, o_ref):
    t_q = q_ref.shape[1]
    p_len = kc_ref.shape[1] // bh
    heads = range(bh)
    sls = [slice(h * hd, (h + 1) * hd) for h in heads]
    q_scale = (hd ** -0.5) * LOG2E
    cqb = cq_ref[0] * LOG2E
    keep = (lax.broadcasted_iota(jnp.int32, (t_q, t_q), 1) <= lax.broadcasted_iota(jnp.int32, (t_q, t_q), 0))
    cq = [cqb[:, f_lane + h:f_lane + h + 1] for h in heads]
    ckc = [ckc_ref[0, f_lane + h:f_lane + h + 1, :] * LOG2E for h in heads]
    ckn = [ckn_ref[0, f_lane + h:f_lane + h + 1, :] * LOG2E for h in heads]
    qb = [(q_ref[0, :, sl] * q_scale).astype(BF16) for sl in sls]
    kc = [kc_ref[0, pl.ds(h, p_len, stride=bh), :].astype(BF16) for h in heads]
    xc = [_dot_nt(a, b) - c for a, b, c in zip(qb, kc, ckc)]
    xn = [jnp.where(keep, _dot_nt(a, kn_ref[0, :, sl].astype(BF16)) - c, NEG_INF)
          for a, sl, c in zip(qb, sls, ckn)]
    m = [c + jnp.maximum(jnp.max(a, axis=1, keepdims=True), jnp.max(b, axis=1, keepdims=True))
         for a, b, c in zip(xc, xn, cq)]
    pc = [jnp.exp2(a + (c - mm)) for a, c, mm in zip(xc, cq, m)]
    pn = [jnp.exp2(a + (c - mm)) for a, c, mm in zip(xn, cq, m)]
    vc = [vc_ref[0, pl.ds(h, p_len, stride=bh), :].astype(BF16) for h in heads]
    acc = [_dot(a.astype(BF16), b) + _dot(c.astype(BF16), vn_ref[0, :, sl].astype(BF16))
           for a, b, c, sl in zip(pc, vc, pn, sls)]
    for h in heads:
        den = jnp.sum(pc[h], axis=1, keepdims=True) + jnp.sum(pn[h], axis=1, keepdims=True)
        o_ref[0, :, sls[h]] = (acc[h] / den).astype(o_ref.dtype)


def _fox_s(q_arr, q_blk, k_new, v_new, k_cache, v_cache, cq_col, ck_cache_row, ck_new_row, *, bh, hd, f_lane):
    b, t_q, bw = k_new.shape
    rows = k_cache.shape[1]
    p_len = rows // bh
    return pl.pallas_call(
        functools.partial(_fox_s_kernel, f_lane, bh, hd),
        grid=(b,),
        in_specs=[
            pl.BlockSpec((1, t_q, bw), lambda i: (i, 0, q_blk)),
            pl.BlockSpec((1, t_q, bw), lambda i: (i, 0, 0)),
            pl.BlockSpec((1, t_q, bw), lambda i: (i, 0, 0)),
            pl.BlockSpec((1, rows, hd), lambda i: (i, 0, 0)),
            pl.BlockSpec((1, rows, hd), lambda i: (i, 0, 0)),
            pl.BlockSpec((1, t_q, LANE), lambda i: (i, 0, 0)),
            pl.BlockSpec((1, LANE, p_len), lambda i: (i, 0, 0)),
            pl.BlockSpec((1, LANE, t_q), lambda i: (i, 0, 0)),
        ],
        out_specs=pl.BlockSpec((1, t_q, bw), lambda i: (i, 0, 0)),
        out_shape=jax.ShapeDtypeStruct((b, t_q, bw), BF16),
        compiler_params=_params(("parallel",)),
        name="fox_s",
    )(q_arr, k_new, v_new, k_cache, v_cache, cq_col, ck_cache_row, ck_new_row)


def _fox_t_kernel(f_lane, hg, hd, k_ref, qt_ref, vt_ref, cq_ref, ck_ref, o_ref, m_scr, l_scr, acc_scr):
    g = pl.program_id(1)
    qi = pl.program_id(2)
    kj = pl.program_id(3)
    tk = k_ref.shape[1]
    tq = qt_ref.shape[1]
    heads = range(hg)
    sls = [slice(h * hd, (h + 1) * hd) for h in heads]
    k_scale = (hd ** -0.5) * LOG2E

    @pl.when(kj == 0)
    def _():
        m_scr[...] = jnp.full(m_scr.shape, NEG_INF, F32)
        l_scr[...] = jnp.zeros(l_scr.shape, F32)
        acc_scr[...] = jnp.zeros(acc_scr.shape, F32)

    def update(masked):
        sub = min(tk, 256)
        strip = min(tq, LANE)
        ckb = ck_ref[0] * LOG2E
        lane = lax.broadcasted_iota(jnp.int32, (sub, LANE), 1)
        cq = [cq_ref[0, pl.ds(f_lane + g * hg + h, 1), :] * LOG2E for h in heads]
        for k0 in range(0, tk, sub):
            ks = slice(k0, k0 + sub)
            ck = [jnp.sum(jnp.where(lane == f_lane + g * hg + h, ckb[ks], 0.0), axis=1, keepdims=True)
                  for h in heads]
            s = [_dot((k_ref[0, ks, sl] * k_scale).astype(BF16), qt_ref[sl, :]) for sl in sls]
            alphas, probs = [], []
            for h in heads:
                m_old = m_scr[h]
                l_old = l_scr[h]
                m_p, l_p, a_p, p_p = [], [], [], []
                for q0 in range(0, tq, strip):
                    ls = slice(q0, q0 + strip)
                    mo = m_old[:, ls]
                    if masked and k0 > q0 + strip - 1:
                        m_p.append(mo)
                        l_p.append(l_old[:, ls])
                        a_p.append(jnp.ones((1, strip), F32))
                        p_p.append(jnp.zeros((sub, strip), BF16))
                        continue
                    x = s[h][:, ls] - ck[h]
                    if masked:
                        kpos = kj * tk + k0 + lax.broadcasted_iota(jnp.int32, (sub, strip), 0)
                        qpos = qi * tq + q0 + lax.broadcasted_iota(jnp.int32, (sub, strip), 1)
                        x = jnp.where(kpos <= qpos, x, NEG_INF)
                    c = cq[h][:, ls]
                    mn = jnp.maximum(mo, c + jnp.max(x, axis=0, keepdims=True))
                    p = jnp.exp2(x + (c - mn))
                    al = jnp.exp2(mo - mn)
                    m_p.append(mn)
                    l_p.append(al * l_old[:, ls] + jnp.sum(p, axis=0, keepdims=True))
                    a_p.append(al)
                    p_p.append(p.astype(BF16))
                m_scr[h] = jnp.concatenate(m_p, axis=1)
                l_scr[h] = jnp.concatenate(l_p, axis=1)
                alphas.append(jnp.concatenate(a_p, axis=1))
                probs.append(jnp.concatenate(p_p, axis=1))
            pv = [_dot(vt_ref[sl, ks], pb) for sl, pb in zip(sls, probs)]
            for h in heads:
                acc_scr[h] = alphas[h] * acc_scr[h] + pv[h]

    active = kj * tk <= qi * tq + (tq - 1)
    crosses = kj * tk + (tk - 1) > qi * tq

    @pl.when(active & crosses)
    def _():
        update(True)

    @pl.when(active & jnp.logical_not(crosses))
    def _():
        update(False)

    @pl.when(kj == pl.num_programs(3) - 1)
    def _():
        for h in heads:
            o_ref[0, :, sls[h]] = (acc_scr[h] / l_scr[h]).T.astype(o_ref.dtype)


def _fox_t(k3, qv_t, cq_row, ck_col, *, bh, hd, f_lane):
    b, t_len, bw = k3.shape
    tq = _tile(t_len, 512)
    tk = tq
    nq = t_len // tq
    hg = 16 if bh % 16 == 0 else bh
    wg = hg * hd
    ng = bh // hg
    last = lambda qi: (qi * tq + (tq - 1)) // tk
    kblk = lambda qi, kj: jnp.minimum(kj, last(qi))
    return pl.pallas_call(
        functools.partial(_fox_t_kernel, f_lane, hg, hd),
        grid=(b, ng, nq, nq),
        in_specs=[
            pl.BlockSpec((1, tk, wg), lambda i, g, qi, kj: (i, kblk(qi, kj), g)),
            pl.BlockSpec((wg, tq), lambda i, g, qi, kj: (g, i * nq + qi)),
            pl.BlockSpec((wg, tk), lambda i, g, qi, kj: (ng + g, i * nq + kblk(qi, kj))),
            pl.BlockSpec((1, LANE, tq), lambda i, g, qi, kj: (i, 0, qi)),
            pl.BlockSpec((1, tk, LANE), lambda i, g, qi, kj: (i, kblk(qi, kj), 0)),
        ],
        out_specs=pl.BlockSpec((1, tq, wg), lambda i, g, qi, kj: (i, qi, g)),
        out_shape=jax.ShapeDtypeStruct((b, t_len, bw), BF16),
        scratch_shapes=[
            pltpu.VMEM((hg, 1, tq), F32),
            pltpu.VMEM((hg, 1, tq), F32),
            pltpu.VMEM((hg, hd, tq), F32),
        ],
        compiler_params=_params(("parallel", "parallel", "parallel", "arbitrary")),
        name="fox_t",
    )(k3, qv_t, qv_t, cq_row, ck_col)


def _layer_norm_rows(x, g, b):
    mu = jnp.mean(x, axis=1, keepdims=True)
    xc = x - mu
    var = jnp.mean(xc * xc, axis=1, keepdims=True)
    return xc * lax.rsqrt(var + LN_EPS) * g + b


def _layer_norm_ref(ref, g_ref, b_ref, rows):
    def body(r, carry):
        sl = pl.ds(pl.multiple_of(r * rows, rows), rows)
        ref[sl, :] = _layer_norm_rows(ref[sl, :], g_ref[...], b_ref[...])
        return carry
    lax.fori_loop(0, ref.shape[0] // rows, body, 0)


def _outproj_kernel(alpha, tn, oa_ref, ob_ref, wa_ref, wb_ref, x_ref, g_ref, b_ref, hid_ref, hidt_ref):
    j = pl.program_id(1)
    col = pl.multiple_of(j * tn, LANE)
    hid_ref[:, pl.ds(col, tn)] = (alpha * x_ref[...] + _dot(oa_ref[...], wa_ref[...])
                                  + _dot(ob_ref[...], wb_ref[...]))

    @pl.when(j == pl.num_programs(1) - 1)
    def _():
        tm, d = hid_ref.shape
        _layer_norm_ref(hid_ref, g_ref, b_ref, min(tm, 32))
        rb = min(tm, LANE)
        step = _tile(d, 512)
        for r in range(tm // rb):
            for c in range(d // step):
                hidt_ref[c * step:(c + 1) * step, r * rb:(r + 1) * rb] = (
                    hid_ref[r * rb:(r + 1) * rb, c * step:(c + 1) * step].T.astype(BF16))


def _outproj(o_a, o_b, w_a, w_b, x2d, g, b, alpha):
    m, d = x2d.shape
    tm = _tile(m, 512)
    tn = _tile(d, 512)
    aw, bw = o_a.shape[1], o_b.shape[1]
    assert w_a.shape[0] == aw + bw and aw % bw == 0
    return pl.pallas_call(
        functools.partial(_outproj_kernel, alpha, tn),
        grid=(m // tm, d // tn),
        in_specs=[
            pl.BlockSpec((tm, o_a.shape[1]), lambda i, j: (i, 0)),
            pl.BlockSpec((tm, o_b.shape[1]), lambda i, j: (i, 0)),
            pl.BlockSpec((aw, tn), lambda i, j: (0, j)),
            pl.BlockSpec((bw, tn), lambda i, j: (aw // bw, j)),
            pl.BlockSpec((tm, tn), lambda i, j: (i, j)),
            pl.BlockSpec((1, d), lambda i, j: (0, 0)),
            pl.BlockSpec((1, d), lambda i, j: (0, 0)),
        ],
        out_specs=[
            pl.BlockSpec((tm, d), lambda i, j: (i, 0)),
            pl.BlockSpec((d, tm), lambda i, j: (0, i)),
        ],
        out_shape=[
            jax.ShapeDtypeStruct((m, d), F32),
            jax.ShapeDtypeStruct((d, m), BF16),
        ],
        compiler_params=_params(("parallel", "arbitrary")),
        name="outproj",
    )(o_a, o_b, w_a, w_b, x2d, g, b)


def _top_values(x, n):
    vals = []
    for _ in range(n):
        m = jnp.max(x, axis=0, keepdims=True)
        vals.append(m)
        x = jnp.where(x == m, NEG_INF, x)
    return vals


def _route_kernel(topk, hp, ht_ref, wq_ref, key_ref, a1_ref, a2_ref, thr_ref, cand_scr):
    dk = key_ref.shape[3]
    n = topk + 1
    pairs = [(a, b) for a in range(n) for b in range(n) if (a + 1) * (b + 1) <= n]

    def scores(h):
        qt = _dot(wq_ref[h * 2 * dk:(h + 1) * 2 * dk, :], ht_ref[...])
        return (_dot(key_ref[h, 0], qt[:dk].astype(BF16)),
                _dot(key_ref[h, 1], qt[dk:].astype(BF16)))

    nxt = scores(0)
    for h in range(hp):
        s1, s2 = nxt
        if h + 1 < hp:
            nxt = scores(h + 1)
        top1 = _top_values(s1, n)
        top2 = _top_values(s2, n)
        cand_scr[...] = jnp.full(cand_scr.shape, NEG_INF, F32)
        for r, (a, b) in enumerate(pairs):
            cand_scr[r:r + 1, :] = top1[a] + top2[b]
        best = _top_values(cand_scr[...], n)
        z = jnp.exp(best[0] - best[0])
        for t in best[1:topk]:
            z = z + jnp.exp(t - best[0])
        a1_ref[h] = jnp.exp(s1 - top1[0]) / z
        a2_ref[h] = jnp.exp(s2 - top2[0])
        thr = jnp.exp(0.5 * (best[topk - 1] + best[topk]) - best[0]) / z
        thr_ref[h] = jnp.broadcast_to(thr, thr_ref.shape[1:])


def _num_candidates(n):
    return -(-sum(n // (a + 1) for a in range(n)) // 8) * 8


def _route(hid_t, wq_t, keys, topk):
    d, m = hid_t.shape
    ph, _, nk, dk = keys.shape
    tm = _tile(m, 512)
    hp = 8 if ph % 8 == 0 else ph
    out = jax.ShapeDtypeStruct((ph, nk, m), F32)
    blk = pl.BlockSpec((hp, nk, tm), lambda i, h: (h, 0, i))
    return pl.pallas_call(
        functools.partial(_route_kernel, topk, hp),
        grid=(m // tm, ph // hp),
        in_specs=[
            pl.BlockSpec((d, tm), lambda i, h: (0, i)),
            pl.BlockSpec((hp * 2 * dk, d), lambda i, h: (h, 0)),
            pl.BlockSpec((hp, 2, nk, dk), lambda i, h: (h, 0, 0, 0)),
        ],
        out_specs=[blk, blk, pl.BlockSpec((hp, 8, tm), lambda i, h: (h, 0, i))],
        out_shape=[out, out, jax.ShapeDtypeStruct((ph, 8, m), F32)],
        scratch_shapes=[pltpu.VMEM((_num_candidates(topk + 1), tm), F32)],
        compiler_params=_params(("parallel", "arbitrary")),
        name="peer_route",
    )(hid_t, wq_t, keys)


def _peer_kernel(ht_ref, u_ref, vt_ref, a1_ref, a2_ref, thr_ref, o_ref, act_scr, pre_scr):
    e = pl.program_id(1)
    n_tiles = pl.num_programs(1) - 1
    ph, nk, tm = a2_ref.shape
    te = u_ref.shape[0]
    n_sub = te // nk
    d = o_ref.shape[0]
    slot = e % 2
    strip = min(tm, LANE)

    @pl.when(e == 0)
    def _():
        o_ref[...] = jnp.zeros(o_ref.shape, F32)
        act_scr[1] = jnp.zeros(act_scr.shape[1:], BF16)

    @pl.when(e < n_tiles)
    def _():
        pre_scr[...] = _dot(u_ref[...], ht_ref[...])

    n_ch = PEER_DRAIN_TRIPS
    step = d // n_ch
    gr = te // n_ch
    for c in range(n_ch):
        sl = slice(c * step, (c + 1) * step)
        o_ref[sl, :] += _dot(vt_ref[sl, :], act_scr[1 - slot])
        for r0 in range(c * gr, (c + 1) * gr, min(gr, nk)):
            rn = min(gr, nk)
            row = jnp.minimum(e * n_sub + r0 // nk, nk - 1)
            j0 = r0 % nk
            a1_rows = [a1_ref[h, pl.ds(row, 1), :] for h in range(ph)]
            for t in range(tm // strip):
                ls = slice(t * strip, (t + 1) * strip)
                gate = None
                for h in range(ph):
                    g = a2_ref[h, j0:j0 + rn, ls] * a1_rows[h][:, ls]
                    term = jnp.where(g >= thr_ref[h, 0:1, ls], g, 0.0)
                    gate = term if gate is None else gate + term
                x = pre_scr[r0:r0 + rn, ls]
                act = gate * (0.5 * x * (1.0 + lax.erf(x * (2.0 ** -0.5))))
                act_scr[slot, r0:r0 + rn, ls] = act.astype(BF16)


def _peer(hid_t, u_b, v_t, a1, a2, thr):
    d, m = hid_t.shape
    ne = u_b.shape[0]
    ph, nk, _ = a1.shape
    tm = _tile(m, 512)
    te = _tile(ne, 512)
    n_tiles = ne // te
    assert te % nk == 0 and d % (te // nk) == 0
    rt = pl.BlockSpec((ph, nk, tm), lambda i, e: (0, 0, i))
    return pl.pallas_call(
        _peer_kernel,
        grid=(m // tm, n_tiles + 1),
        in_specs=[
            pl.BlockSpec((d, tm), lambda i, e: (0, i)),
            pl.BlockSpec((te, d), lambda i, e: (jnp.minimum(e, n_tiles - 1), 0)),
            pl.BlockSpec((d, te), lambda i, e: (0, jnp.maximum(e - 1, 0))),
            rt, rt,
            pl.BlockSpec((ph, 8, tm), lambda i, e: (0, 0, i)),
        ],
        out_specs=pl.BlockSpec((d, tm), lambda i, e: (0, i)),
        out_shape=jax.ShapeDtypeStruct((d, m), F32),
        scratch_shapes=[pltpu.VMEM((2, te, tm), BF16), pltpu.VMEM((te, tm), F32)],
        compiler_params=_params(("parallel", "arbitrary")),
        name="peer_dense",
    )(hid_t, u_b, v_t, a1, a2, thr)


def _final_kernel(alpha, hid_ref, pt_ref, g_ref, b_ref, y_ref):
    d = hid_ref.shape[1]
    step = _tile(d, 512)
    for c in range(d // step):
        sl = slice(c * step, (c + 1) * step)
        y_ref[:, sl] = alpha * hid_ref[:, sl] + pt_ref[sl, :].T
    _layer_norm_ref(y_ref, g_ref, b_ref, min(y_ref.shape[0], 32))


def _final(hid, peer_t, g, b, alpha):
    m, d = hid.shape
    tm = _tile(m, 256)
    return pl.pallas_call(
        functools.partial(_final_kernel, alpha),
        grid=(m // tm,),
        in_specs=[
            pl.BlockSpec((tm, d), lambda i: (i, 0)),
            pl.BlockSpec((d, tm), lambda i: (0, i)),
            pl.BlockSpec((1, d), lambda i: (0, 0)),
            pl.BlockSpec((1, d), lambda i: (0, 0)),
        ],
        out_specs=pl.BlockSpec((tm, d), lambda i: (i, 0)),
        out_shape=jax.ShapeDtypeStruct((m, d), F32),
        compiler_params=_params(("parallel",)),
        name="final_ln",
    )(hid, peer_t, g, b)


def _pad_lanes(a):
    return jnp.pad(a, [(0, 0)] * (a.ndim - 1) + [(0, LANE - a.shape[-1])])


def _trunk(x, conv_hist, s0, fox_cache, wts, depth):
    (w_main, w_small, conv_w, prow, pcol, norm_w, w_a, w_b, ln1_g, ln1_b, wq_t, keys, u_b, v_t,
     ln2_g, ln2_b, ah, bh, hd, topk) = wts
    b, t_len, d = x.shape
    m = b * t_len
    aw, bw = ah * hd, bh * hd
    alpha = (2 * depth) ** 0.25
    f_lane = 2 * ah
    x2d = x.reshape(m, d)

    n_a = 4 * aw
    prompt = fox_cache is None
    proj, k_new, v_new, sm, smt, *qv_t = _inproj(x2d, w_main, w_small, n_a, bw, prompt)
    proj3 = proj.reshape(b, t_len, -1)
    k_new = k_new.reshape(b, t_len, bw)
    v_new = v_new.reshape(b, t_len, bw)

    c_len = min(64, t_len)
    zero_c = jnp.zeros((b, 1, LANE), F32)
    zero_r = jnp.zeros((b, LANE, 1), F32)
    if fox_cache is None:
        carry_c, carry_r = zero_c, zero_r
    else:
        clf = fox_cache[2].astype(F32)
        p_len = clf.shape[1]
        clf_col = jnp.pad(clf, ((0, 0), (0, 0), (f_lane, LANE - f_lane - bh)))
        clf_row = jnp.swapaxes(clf_col, 1, 2)
        cc_col, _, cc_row = _gates(clf_col, clf_row, zero_c, zero_r, prow, pcol,
                                   apply=False, c_len=min(64, p_len), ah=ah, bh=bh)
        carry_c = cc_col[:, p_len - 1:, :]
        carry_r = cc_row[:, :, p_len - 1:]
    smt3 = jnp.swapaxes(smt.reshape(LANE, b, t_len), 0, 1)
    col, rowc, rowf = _gates(sm.reshape(b, t_len, LANE), smt3, carry_c, carry_r, prow, pcol,
                             apply=True, c_len=c_len, ah=ah, bh=bh)
    logf = col[:, :, f_lane + bh:f_lane + 2 * bh]

    o_a, s_new = _gdn(proj3, conv_hist, conv_w, col, rowc, s0, norm_w, c_len=c_len, ah=ah, hd=hd)
    conv_new = jnp.concatenate([conv_hist.astype(F32), proj3[:, :, :3 * aw]], axis=1)[:, -3:]

    qb0 = n_a // hd
    if prompt:
        o_b = _fox_t(k_new, qv_t[0], rowf, col, bh=bh, hd=hd, f_lane=f_lane)
    else:
        assert n_a % bw == 0
        k_cache = fox_cache[0].reshape(b, -1, hd).astype(F32)
        v_cache = fox_cache[1].reshape(b, -1, hd).astype(F32)
        o_b = _fox_s(proj3, n_a // bw, k_new, v_new, k_cache, v_cache, col, cc_row, rowf,
                     bh=bh, hd=hd, f_lane=f_lane)

    hid, hid_t = _outproj(o_a.reshape(m, aw), o_b.reshape(m, bw), w_a, w_b, x2d, ln1_g, ln1_b, alpha)
    a1, a2, thr = _route(hid_t, wq_t, keys, topk)
    peer_t = _peer(hid_t, u_b, v_t, a1, a2, thr)
    y = _final(hid, peer_t, ln2_g, ln2_b, alpha).reshape(b, t_len, d)
    return y, (k_new.reshape(b, t_len, bh, hd), v_new.reshape(b, t_len, bh, hd), logf, s_new, conv_new)


def kernel(x_prompt, x_sample, cache_fox_k, cache_fox_v, cache_fox_logf, state_gdn, state_gdn_conv,
           w_in, gdn_conv_w, gdn_a_log, gdn_dt_bias, gdn_norm_w, fox_f_bias, w_out, ln1_g, ln1_b,
           peer_w_q, peer_sub_keys, peer_u, peer_v, ln2_g, ln2_b):
    depth = w_in.shape[0]
    ah = gdn_a_log.shape[1]
    bh = fox_f_bias.shape[1]
    hd = gdn_norm_w.shape[1]
    aw, bw = ah * hd, bh * hd
    topk = 16
    assert 2 * ah + 2 * bh <= LANE
    n_p = x_prompt.shape[0]
    yp, ys = x_prompt, x_sample
    outs_p, outs_s = [], []
    for l in range(depth):
        o_a_a = 4 * aw
        o_b_qkv = o_a_a + 2 * ah
        o_b_f = o_b_qkv + 3 * bw
        wl = w_in[l]
        w_main = jnp.concatenate([wl[:, :o_a_a].astype(BF16), wl[:, o_b_qkv:o_b_f].astype(BF16)], axis=1)
        w_out_b = w_out[l].astype(BF16)
        w_f = wl[:, o_b_f:o_b_f + bh]
        w_small = _pad_lanes(jnp.concatenate([wl[:, o_a_a:o_b_qkv], w_f, w_f], axis=1)).astype(BF16)
        zeros_a = jnp.zeros((ah,), F32)
        prow = jnp.stack([
            _pad_lanes(gdn_a_log[l].astype(F32)),
            _pad_lanes(jnp.concatenate([gdn_dt_bias[l].astype(F32), zeros_a,
                                        fox_f_bias[l].astype(F32), fox_f_bias[l].astype(F32)])),
        ])
        wts = (w_main, w_small, gdn_conv_w[l].astype(F32), prow, prow.T,
               gdn_norm_w[l].reshape(1, hd).astype(F32),
               w_out_b, w_out_b,
               ln1_g[l].reshape(1, -1), ln1_b[l].reshape(1, -1),
               peer_w_q[l].T.astype(BF16), peer_sub_keys[l].astype(BF16),
               peer_u[l].astype(BF16), peer_v[l].T.astype(BF16),
               ln2_g[l].reshape(1, -1), ln2_b[l].reshape(1, -1), ah, bh, hd, topk)
        conv0 = jnp.zeros((n_p, 3, 3 * aw), yp.dtype)
        s0 = jnp.zeros((n_p, ah, hd, hd), yp.dtype)
        yp, st_p = _trunk(yp, conv0, s0, None, wts, depth)
        ys, st_s = _trunk(ys, state_gdn_conv[l], state_gdn[l],
                          (cache_fox_k[l], cache_fox_v[l], cache_fox_logf[l]), wts, depth)
        outs_p.append(st_p)
        outs_s.append(st_s)
    stack = lambda outs, n: jnp.stack([o[n] for o in outs], axis=0)
    return ((yp, ys) + tuple(stack(outs_p, n) for n in range(5))
            + tuple(stack(outs_s, n) for n in range(5)))
```
